```python
import math
import jax, jax.numpy as jnp
from jax import lax
import numpy as np

D_MODEL = 4096
BATCH = 4
SEQ = 2048
DEPTH = 2
DEC_BATCH = 128
DEC_SEQ = 1
PAST_LEN = 16384
PAGE_SIZE = 128

MIX_WIDTH = D_MODEL
CONV_CH = MIX_WIDTH // 2
SSM_CH = MIX_WIDTH - CONV_CH
SSM_GROUP = 16
SSM_GROUPS = SSM_CH // SSM_GROUP
SSM_STATE = 64
SSM_CHUNK = 128
CONV_WIDTH = 31
N_MEM = 256
XA_HEADS = 4
XA_HEAD_DIM = 128
XA_WIDTH = XA_HEADS * XA_HEAD_DIM
D_FF = 4 * D_MODEL
EPS = 1e-6

kernel_name = 'hybrid_conv_s5_memxattn_decoder_step'


def _rmsnorm(x, g):
    xf = x.astype(jnp.float32)
    y = xf * lax.rsqrt(jnp.mean(xf * xf, axis=-1, keepdims=True) + EPS)
    return (y * g.astype(jnp.float32)).astype(x.dtype)


def _layernorm(x, g, b):
    xf = x.astype(jnp.float32)
    xc = xf - jnp.mean(xf, axis=-1, keepdims=True)
    var = jnp.mean(xc * xc, axis=-1, keepdims=True)
    return (xc * lax.rsqrt(var + EPS) * g.astype(jnp.float32) + b.astype(jnp.float32)).astype(x.dtype)


def _ssm_discretise(a_re, a_im, log_dt, b_re, b_im, c_re, c_im):
    a_re = a_re.astype(jnp.float32)
    a_im = a_im.astype(jnp.float32)
    b_re = b_re.astype(jnp.float32)
    b_im = b_im.astype(jnp.float32)
    dt = jnp.exp(log_dt.astype(jnp.float32))[:, None]
    mag = jnp.exp(a_re * dt)
    ang = a_im * dt
    abar_re = mag * jnp.cos(ang)
    abar_im = mag * jnp.sin(ang)
    nr = abar_re - 1.0
    ni = abar_im
    den = a_re * a_re + a_im * a_im
    cr = (nr * a_re + ni * a_im) / den
    ci = (ni * a_re - nr * a_im) / den
    bbar_re = cr[..., None] * b_re - ci[..., None] * b_im
    bbar_im = cr[..., None] * b_im + ci[..., None] * b_re
    return (abar_re, abar_im, bbar_re, bbar_im,
            c_re.astype(jnp.float32), c_im.astype(jnp.float32))


def _cmul_combine(e1, e2):
    ar1, ai1, br1, bi1 = e1
    ar2, ai2, br2, bi2 = e2
    return (ar2 * ar1 - ai2 * ai1,
            ar2 * ai1 + ai2 * ar1,
            ar2 * br1 - ai2 * bi1 + br2,
            ar2 * bi1 + ai2 * br1 + bi2)


def _ssm_block(u, h_re, h_im, abar_re, abar_im, bbar_re, bbar_im, c_re, c_im):
    bu_re = jnp.einsum('blgp,gnp->blgn', u, bbar_re)
    bu_im = jnp.einsum('blgp,gnp->blgn', u, bbar_im)
    a_re = jnp.broadcast_to(abar_re, bu_re.shape)
    a_im = jnp.broadcast_to(abar_im, bu_im.shape)
    pr, pi, sr, si = lax.associative_scan(_cmul_combine, (a_re, a_im, bu_re, bu_im), axis=1)
    hr = pr * h_re[:, None] - pi * h_im[:, None] + sr
    hi = pr * h_im[:, None] + pi * h_re[:, None] + si
    y = jnp.einsum('blgn,gpn->blgp', hr, c_re) - jnp.einsum('blgn,gpn->blgp', hi, c_im)
    return y, hr[:, -1], hi[:, -1]


def _ssm_scan(u, h_re, h_im, consts):
    nb, nl = u.shape[0], u.shape[1]
    if nl > SSM_CHUNK and nl % SSM_CHUNK == 0:
        uc = u.reshape(nb, nl // SSM_CHUNK, SSM_CHUNK, SSM_GROUPS, SSM_GROUP).swapaxes(0, 1)

        def step(carry, u_blk):
            y_blk, hr_, hi_ = _ssm_block(u_blk, carry[0], carry[1], *consts)
            return (hr_, hi_), y_blk

        (hr, hi), ys = lax.scan(step, (h_re, h_im), uc)
        return ys.swapaxes(0, 1).reshape(u.shape), hr, hi
    return _ssm_block(u, h_re, h_im, *consts)


def _mixer(h, conv_buf, s_re, s_im, p):
    nb, nl = h.shape[0], h.shape[1]
    z = h @ p['w_in']
    a_val = z[..., :CONV_CH]
    a_gate = z[..., CONV_CH:2 * CONV_CH]
    s_in = z[..., 2 * CONV_CH:]
    g = a_val * jax.nn.sigmoid(a_gate)
    ext = jnp.concatenate([conv_buf.astype(g.dtype), g], axis=1)
    new_buf = ext[:, -(CONV_WIDTH - 1):]
    c = lax.conv_general_dilated(ext, p['conv_w'][:, None, :].astype(g.dtype), (1,), 'VALID',
                                 dimension_numbers=('NWC', 'WIO', 'NWC'),
                                 feature_group_count=CONV_CH) + p['conv_b']
    c = jax.nn.silu(_layernorm(c, p['conv_ln_g'], p['conv_ln_b']))
    consts = _ssm_discretise(p['ssm_a_re'], p['ssm_a_im'], p['ssm_log_dt'],
                             p['ssm_b_re'], p['ssm_b_im'], p['ssm_c_re'], p['ssm_c_im'])
    u = s_in.astype(jnp.float32).reshape(nb, nl, SSM_GROUPS, SSM_GROUP)
    y, hr, hi = _ssm_scan(u, s_re.astype(jnp.float32), s_im.astype(jnp.float32), consts)
    y = (y.reshape(nb, nl, SSM_CH) + p['ssm_d'].astype(jnp.float32) * s_in.astype(jnp.float32)).astype(h.dtype)
    gy = jax.nn.gelu(y)
    s_out = gy * jax.nn.sigmoid(gy @ p['w_glu'])
    merged = jnp.concatenate([_rmsnorm(c, p['branch_g_conv']),
                              _rmsnorm(s_out, p['branch_g_ssm'])], axis=-1)
    return merged @ p['w_out'], new_buf, hr, hi


def _mem_kv(mem, p):
    nb = mem.shape[0]
    m = _rmsnorm(mem, p['norm_mem_g'])
    k = (m @ p['w_xk']).reshape(nb, N_MEM, XA_HEADS, XA_HEAD_DIM)
    v = (m @ p['w_xv']).reshape(nb, N_MEM, XA_HEADS, XA_HEAD_DIM)
    return k, v


def _cross_attn(h, k, v, p):
    nb, nl = h.shape[0], h.shape[1]
    q = (h @ p['w_xq']).reshape(nb, nl, XA_HEADS, XA_HEAD_DIM)
    s = jnp.einsum('blhd,bmhd->bhlm', q, k.astype(q.dtype)).astype(jnp.float32) * (XA_HEAD_DIM ** -0.5)
    w = jax.nn.softmax(s, axis=-1).astype(h.dtype)
    o = jnp.einsum('bhlm,bmhd->blhd', w, v.astype(h.dtype)).reshape(nb, nl, XA_WIDTH)
    return o @ p['w_xo']


def _layer(x, conv_buf, s_re, s_im, mk, mv, p):
    m, new_buf, hr, hi = _mixer(_rmsnorm(x, p['norm_mix_g']), conv_buf, s_re, s_im, p)
    x = x + m
    x = x + _cross_attn(_rmsnorm(x, p['norm_x_g']), mk, mv, p)
    hh = _rmsnorm(x, p['norm_ffn_g'])
    x = x + jnp.square(jax.nn.relu(hh @ p['w_up'])) @ p['w_down']
    return x, new_buf, hr, hi


def setup_inputs(seed: int = 0) -> dict:
    key = jax.random.key(seed)
    ks = iter(jax.random.split(key, 48))
    f32 = jnp.float32

    def nrm(shape, scale):
        return jax.random.normal(next(ks), shape, f32) * scale

    def gain(shape):
        return 1.0 + nrm(shape, 0.02)

    L = DEPTH
    x_prompt = nrm((BATCH, SEQ, D_MODEL), 1.0)
    x_sample = nrm((DEC_BATCH, DEC_SEQ, D_MODEL), 1.0)
    mem_prompt = nrm((BATCH, N_MEM, D_MODEL), 1.0)
    cache_conv = nrm((L, DEC_BATCH, CONV_WIDTH - 1, CONV_CH), 0.5)
    state_ssm_re = nrm((L, DEC_BATCH, SSM_GROUPS, SSM_STATE), 0.1)
    state_ssm_im = nrm((L, DEC_BATCH, SSM_GROUPS, SSM_STATE), 0.1)
    cache_mem_k = nrm((L, DEC_BATCH, N_MEM, XA_HEADS, XA_HEAD_DIM), 1.0)
    cache_mem_v = nrm((L, DEC_BATCH, N_MEM, XA_HEADS, XA_HEAD_DIM), 1.0)
    n_idx = jnp.arange(SSM_STATE, dtype=f32)
    return {
        'x_prompt': x_prompt,
        'x_sample': x_sample,
        'mem_prompt': mem_prompt,
        'cache_conv': cache_conv,
        'state_ssm_re': state_ssm_re,
        'state_ssm_im': state_ssm_im,
        'cache_mem_k': cache_mem_k,
        'cache_mem_v': cache_mem_v,
        'norm_mix_g': gain((L, D_MODEL)),
        'w_in': nrm((L, D_MODEL, 2 * CONV_CH + SSM_CH), D_MODEL ** -0.5),
        'conv_w': nrm((L, CONV_WIDTH, CONV_CH), CONV_WIDTH ** -0.5),
        'conv_b': nrm((L, CONV_CH), 0.02),
        'conv_ln_g': gain((L, CONV_CH)),
        'conv_ln_b': nrm((L, CONV_CH), 0.02),
        'ssm_a_re': -0.5 + nrm((L, SSM_GROUPS, SSM_STATE), 0.01),
        'ssm_a_im': jnp.pi * n_idx + nrm((L, SSM_GROUPS, SSM_STATE), 0.01),
        'ssm_log_dt': jax.random.uniform(next(ks), (L, SSM_GROUPS), f32,
                                         minval=math.log(1e-3), maxval=math.log(1e-1)),
        'ssm_b_re': nrm((L, SSM_GROUPS, SSM_STATE, SSM_GROUP), SSM_GROUP ** -0.5),
        'ssm_b_im': nrm((L, SSM_GROUPS, SSM_STATE, SSM_GROUP), SSM_GROUP ** -0.5),
        'ssm_c_re': nrm((L, SSM_GROUPS, SSM_GROUP, SSM_STATE), (2 * SSM_STATE) ** -0.5),
        'ssm_c_im': nrm((L, SSM_GROUPS, SSM_GROUP, SSM_STATE), (2 * SSM_STATE) ** -0.5),
        'ssm_d': nrm((L, SSM_CH), 1.0),
        'w_glu': nrm((L, SSM_CH, SSM_CH), SSM_CH ** -0.5),
        'branch_g_conv': gain((L, CONV_CH)),
        'branch_g_ssm': gain((L, SSM_CH)),
        'w_out': nrm((L, MIX_WIDTH, D_MODEL), MIX_WIDTH ** -0.5),
        'norm_x_g': gain((L, D_MODEL)),
        'norm_mem_g': gain((L, D_MODEL)),
        'w_xq': nrm((L, D_MODEL, XA_WIDTH), D_MODEL ** -0.5),
        'w_xk': nrm((L, D_MODEL, XA_WIDTH), D_MODEL ** -0.5),
        'w_xv': nrm((L, D_MODEL, XA_WIDTH), D_MODEL ** -0.5),
        'w_xo': nrm((L, XA_WIDTH, D_MODEL), XA_WIDTH ** -0.5),
        'norm_ffn_g': gain((L, D_MODEL)),
        'w_up': nrm((L, D_MODEL, D_FF), D_MODEL ** -0.5),
        'w_down': nrm((L, D_FF, D_MODEL), D_FF ** -0.5),
        'norm_final_g': gain((D_MODEL,)),
    }


def reference(x_prompt, x_sample, mem_prompt, cache_conv, state_ssm_re, state_ssm_im,
              cache_mem_k, cache_mem_v, norm_mix_g, w_in, conv_w, conv_b, conv_ln_g, conv_ln_b,
              ssm_a_re, ssm_a_im, ssm_log_dt, ssm_b_re, ssm_b_im, ssm_c_re, ssm_c_im, ssm_d,
              w_glu, branch_g_conv, branch_g_ssm, w_out, norm_x_g, norm_mem_g, w_xq, w_xk, w_xv,
              w_xo, norm_ffn_g, w_up, w_down, norm_final_g):
    xp = x_prompt
    xs = x_sample
    nbp = x_prompt.shape[0]
    mk_p_all, mv_p_all, cb_p_all, sr_p_all, si_p_all = [], [], [], [], []
    cb_s_all, sr_s_all, si_s_all = [], [], []
    for l in range(DEPTH):
        p = {
            'norm_mix_g': norm_mix_g[l], 'w_in': w_in[l], 'conv_w': conv_w[l], 'conv_b': conv_b[l],
            'conv_ln_g': conv_ln_g[l], 'conv_ln_b': conv_ln_b[l],
            'ssm_a_re': ssm_a_re[l], 'ssm_a_im': ssm_a_im[l], 'ssm_log_dt': ssm_log_dt[l],
            'ssm_b_re': ssm_b_re[l], 'ssm_b_im': ssm_b_im[l], 'ssm_c_re': ssm_c_re[l],
            'ssm_c_im': ssm_c_im[l], 'ssm_d': ssm_d[l], 'w_glu': w_glu[l],
            'branch_g_conv': branch_g_conv[l], 'branch_g_ssm': branch_g_ssm[l], 'w_out': w_out[l],
            'norm_x_g': norm_x_g[l], 'norm_mem_g': norm_mem_g[l], 'w_xq': w_xq[l], 'w_xk': w_xk[l],
            'w_xv': w_xv[l], 'w_xo': w_xo[l], 'norm_ffn_g': norm_ffn_g[l], 'w_up': w_up[l],
            'w_down': w_down[l],
        }
        mk_p, mv_p = _mem_kv(mem_prompt, p)
        zero_buf = jnp.zeros((nbp, CONV_WIDTH - 1, CONV_CH), xp.dtype)
        zero_state = jnp.zeros((nbp, SSM_GROUPS, SSM_STATE), jnp.float32)
        xp, cb_p, sr_p, si_p = _layer(xp, zero_buf, zero_state, zero_state, mk_p, mv_p, p)
        xs, cb_s, sr_s, si_s = _layer(xs, cache_conv[l], state_ssm_re[l], state_ssm_im[l],
                                      cache_mem_k[l], cache_mem_v[l], p)
        mk_p_all.append(mk_p)
        mv_p_all.append(mv_p)
        cb_p_all.append(cb_p)
        sr_p_all.append(sr_p)
        si_p_all.append(si_p)
        cb_s_all.append(cb_s)
        sr_s_all.append(sr_s)
        si_s_all.append(si_s)
    y_prompt = _rmsnorm(xp, norm_final_g)
    y_sample = _rmsnorm(xs, norm_final_g)
    return (y_prompt, y_sample,
            jnp.stack(mk_p_all), jnp.stack(mv_p_all), jnp.stack(cb_p_all),
            jnp.stack(sr_p_all), jnp.stack(si_p_all),
            jnp.stack(cb_s_all), jnp.stack(sr_s_all), jnp.stack(si_s_all))
```

```python
import functools

import jax
import jax.numpy as jnp
from jax import lax
from jax.experimental import pallas as pl
from jax.experimental.pallas import tpu as pltpu

F32 = jnp.float32
BF16 = jnp.bfloat16
EPS = 1e-6

V7X_VMEM_LIMIT_BYTES = 56 * 1024 * 1024
LANES = 128
NORM_CHUNK = 32

CONV_W = 31
CONV_HALO = 32
CONV_T = 64

SSM_T = 16
SSM_GB = 8
SSM_P = 16
SSM_N = 64
SSM_SW = 2 * SSM_GB * SSM_N


def _cparams(sem):
    return pltpu.CompilerParams(dimension_semantics=sem,
                                vmem_limit_bytes=V7X_VMEM_LIMIT_BYTES)


def _rms_rows(x, g):
    ms = jnp.mean(x * x, axis=-1, keepdims=True)
    return x * lax.rsqrt(ms + EPS) * g


def _norm_rows_to(x_ref, gain_ref, hn_ref, rows):
    gain = gain_ref[...]

    def body(c, carry):
        r = pl.multiple_of(c * NORM_CHUNK, NORM_CHUNK)
        hn_ref[pl.ds(r, NORM_CHUNK), :] = _rms_rows(
            x_ref[pl.ds(r, NORM_CHUNK), :], gain).astype(hn_ref.dtype)
        return carry

    lax.fori_loop(0, rows // NORM_CHUNK, body, 0)


def _pick_tile(n, target, mult):
    best = None
    for t in range(mult, min(n, target) + 1, mult):
        if n % t == 0:
            best = t
    if best is None:
        raise ValueError(f"no tile for extent {n} (multiple of {mult}, <= {target})")
    return best


def _nt_dot(a, b):
    return lax.dot_general(a, b, (((1,), (1,)), ((), ())),
                           preferred_element_type=F32)


def _in_proj_kernel(x_ref, gain_ref, wv_ref, wg_ref, ws_ref, g_ref, s_ref, hn_ref):
    @pl.when(pl.program_id(1) == 0)
    def _():
        _norm_rows_to(x_ref, gain_ref, hn_ref, x_ref.shape[0])

    hn = hn_ref[...]
    val = jnp.dot(hn, wv_ref[...], preferred_element_type=F32)
    gate = jnp.dot(hn, wg_ref[...], preferred_element_type=F32)
    g_ref[...] = val * jax.nn.sigmoid(gate)
    s_ref[...] = jnp.dot(hn, ws_ref[...], preferred_element_type=F32)


def _in_proj(x, gain, w_in, l, *, tm, tn):
    rows, d = x.shape
    ch = w_in.shape[2] // 3
    nj = ch // tn
    return pl.pallas_call(
        _in_proj_kernel,
        grid=(rows // tm, nj),
        in_specs=[
            pl.BlockSpec((tm, d), lambda i, j: (i, 0)),
            pl.BlockSpec((None, 1, d), lambda i, j: (l, 0, 0)),
            pl.BlockSpec((None, d, tn), lambda i, j: (l, 0, j)),
            pl.BlockSpec((None, d, tn), lambda i, j: (l, 0, nj + j)),
            pl.BlockSpec((None, d, tn), lambda i, j: (l, 0, 2 * nj + j)),
        ],
        out_specs=[
            pl.BlockSpec((tm, tn), lambda i, j: (i, j)),
            pl.BlockSpec((tm, tn), lambda i, j: (i, j)),
        ],
        out_shape=[jax.ShapeDtypeStruct((rows, ch), F32),
                   jax.ShapeDtypeStruct((rows, ch), F32)],
        scratch_shapes=[pltpu.VMEM((tm, d), BF16)],
        compiler_params=_cparams(("parallel", "arbitrary")),
        name="in_proj",
    )(x, gain, w_in, w_in, w_in)


def _conv_tail(c, b_ref, lng_ref, lnb_ref, bg_ref):
    c = c + b_ref[...]
    xc = c - jnp.mean(c, axis=-1, keepdims=True)
    var = jnp.mean(xc * xc, axis=-1, keepdims=True)
    y = xc * lax.rsqrt(var + EPS) * lng_ref[...] + lnb_ref[...]
    y = y * jax.nn.sigmoid(y)
    return _rms_rows(y, bg_ref[...])


def _conv_prompt_kernel(cur_ref, prev_ref, w_ref, b_ref, lng_ref, lnb_ref, bg_ref,
                        o_ref, ext_ref, c_ref):
    t = cur_ref.shape[0]
    first = pl.program_id(1) == 0
    ext_ref[0:CONV_HALO, :] = jnp.where(first, 0.0, prev_ref[...])
    ext_ref[CONV_HALO:CONV_HALO + t, :] = cur_ref[...]
    off = CONV_HALO - (CONV_W - 1)
    for rb in range(t // 8):
        acc = jnp.zeros((8, cur_ref.shape[1]), F32)
        for k in range(CONV_W):
            r0 = rb * 8 + off + k
            acc = acc + w_ref[k:k + 1, :] * ext_ref[r0:r0 + 8, :]
        c_ref[rb * 8:(rb + 1) * 8, :] = acc
    o_ref[...] = _conv_tail(c_ref[...], b_ref, lng_ref, lnb_ref, bg_ref).astype(o_ref.dtype)


def _conv_prompt(g_all, conv_w, conv_b, ln_g, ln_b, bg, l, *, nb, seq):
    rows, ch = g_all.shape
    t = CONV_T
    per_seq = seq // t
    vec = lambda: pl.BlockSpec((None, 1, ch), lambda b, i: (l, 0, 0))
    return pl.pallas_call(
        _conv_prompt_kernel,
        grid=(nb, per_seq),
        in_specs=[
            pl.BlockSpec((t, ch), lambda b, i: (b * per_seq + i, 0)),
            pl.BlockSpec((CONV_HALO, ch),
                         lambda b, i: (jnp.maximum((b * seq + i * t) // CONV_HALO - 1, 0), 0)),
            pl.BlockSpec((None, CONV_W, ch), lambda b, i: (l, 0, 0)),
            vec(), vec(), vec(), vec(),
        ],
        out_specs=pl.BlockSpec((t, ch), lambda b, i: (b * per_seq + i, 0)),
        out_shape=jax.ShapeDtypeStruct((rows, ch), BF16),
        scratch_shapes=[pltpu.VMEM((CONV_HALO + t, ch), F32), pltpu.VMEM((t, ch), F32)],
        compiler_params=_cparams(("parallel", "arbitrary")),
        name="conv_prompt",
    )(g_all, g_all, conv_w, conv_b, ln_g, ln_b, bg)


def _conv_sample_kernel(cache_ref, g_ref, w_ref, b_ref, lng_ref, lnb_ref, bg_ref, cn_any_ref,
                        cache_out_ref, cn_ref):
    del cn_any_ref
    hist = CONV_W - 1
    cache = cache_ref[...]
    g = g_ref[...]
    c = jnp.sum(cache * w_ref[0:hist, :][None], axis=1) + w_ref[hist:CONV_W, :] * g
    cn_ref[...] = _conv_tail(c, b_ref, lng_ref, lnb_ref, bg_ref).astype(cn_ref.dtype)
    cache_out_ref[:, 0:hist - 1, :] = cache[:, 1:hist, :]
    cache_out_ref[:, hist - 1, :] = g


def _conv_sample(cache_conv, g_all, conv_w, conv_b, ln_g, ln_b, bg, cn_all, l, *, row0, bs):
    _, ns, hist, ch = cache_conv.shape
    vec = lambda: pl.BlockSpec((None, 1, ch), lambda j: (l, 0, 0))
    return pl.pallas_call(
        _conv_sample_kernel,
        grid=(ns // bs,),
        in_specs=[
            pl.BlockSpec((None, bs, hist, ch), lambda j: (l, j, 0, 0)),
            pl.BlockSpec((bs, ch), lambda j: (row0 // bs + j, 0)),
            pl.BlockSpec((None, CONV_W, ch), lambda j: (l, 0, 0)),
            vec(), vec(), vec(), vec(),
            pl.BlockSpec(memory_space=pl.ANY),
        ],
        out_specs=[
            pl.BlockSpec((bs, hist, ch), lambda j: (j, 0, 0)),
            pl.BlockSpec((bs, ch), lambda j: (row0 // bs + j, 0)),
        ],
        out_shape=[jax.ShapeDtypeStruct((ns, hist, ch), F32),
                   jax.ShapeDtypeStruct(cn_all.shape, cn_all.dtype)],
        input_output_aliases={7: 1},
        compiler_params=_cparams(("parallel",)),
        name="conv_sample",
    )(cache_conv, g_all, conv_w, conv_b, ln_g, ln_b, bg, cn_all)


def _ssm_kernel(u_ref, are_ref, aim_ref, ldt_ref, b1_ref, b2_ref, c1_ref, c2_ref, d_ref,
                hre_in_ref, him_in_ref,
                gy_ref, hre_p_ref, him_p_ref, hre_s_ref, him_s_ref,
                ws_ref, wit_ref, dst_ref, x_ref, s_ref, hp_ref, hpb_ref, *, nb, nc, ns):
    t_len = SSM_T
    half = SSM_SW // 2
    rows_c = nb * nc
    row_s = rows_c * t_len

    a_re = are_ref[...]
    a_im = aim_ref[...]
    dt = jnp.exp(ldt_ref[...])
    mag = jnp.exp(a_re * dt)
    ang = a_im * dt
    abar_r = mag * jnp.cos(ang)
    abar_i = mag * jnp.sin(ang)
    nr = abar_r - 1.0
    ni = abar_i
    den = a_re * a_re + a_im * a_im
    cf_r = (nr * a_re + ni * a_im) / den
    cf_i = (ni * a_re - nr * a_im) / den

    pw = [(jnp.ones_like(abar_r), jnp.zeros_like(abar_r))]
    for _ in range(t_len):
        pr, pi = pw[-1]
        pw.append((pr * abar_r - pi * abar_i, pr * abar_i + pi * abar_r))

    b1 = b1_ref[...]
    b2 = b2_ref[...]
    c1 = c1_ref[...]
    c2 = c2_ref[...]
    for t in range(t_len):
        sl = slice(t * LANES, (t + 1) * LANES)
        pr, pi = pw[t_len - 1 - t]
        acr = pr * cf_r - pi * cf_i
        aci = pr * cf_i + pi * cf_r
        ws_t = acr * b1 + aci * b2
        ws_ref[sl, :] = ws_t.astype(BF16)
        dst_ref[sl, :] = lax.dot_general(
            ws_t, c1, (((1,), (1,)), ((), ())), precision=lax.Precision.HIGHEST,
            preferred_element_type=F32).astype(BF16)
        pr, pi = pw[t + 1]
        wit_ref[sl, :] = (pr * c1 + pi * c2).astype(BF16)
        x_ref[:, sl] = u_ref[pl.ds(t, rows_c, stride=t_len), :].astype(BF16)

    s_ref[...] = jnp.dot(x_ref[...], ws_ref[...], preferred_element_type=F32)
    at_r = pw[t_len][0][:, :half]
    at_i = pw[t_len][1][:, :half]

    def scan_body(c, carry):
        new = []
        for b in range(nb):
            hr, hi = carry[b]
            row = pl.ds(b * nc + c, 1)
            hp_ref[row, 0:half] = hr
            hp_ref[row, half:] = hi
            sr = s_ref[row, 0:half]
            si = s_ref[row, half:]
            new.append((at_r * hr - at_i * hi + sr, at_r * hi + at_i * hr + si))
        return tuple(new)

    zero = jnp.zeros((1, half), F32)
    final = lax.fori_loop(0, nc, scan_body, tuple((zero, zero) for _ in range(nb)))
    for b in range(nb):
        hre_p_ref[b:b + 1, :] = final[b][0]
        him_p_ref[b:b + 1, :] = final[b][1]
    hpb_ref[...] = hp_ref[...].astype(BF16)

    d_skip = d_ref[...]
    for t in range(t_len):
        sl = slice(t * LANES, (t + 1) * LANES)
        y = jnp.dot(x_ref[:, 0:(t + 1) * LANES], dst_ref[(t_len - 1 - t) * LANES:, :],
                    preferred_element_type=F32)
        y = y + _nt_dot(hpb_ref[...], wit_ref[sl, :])
        u_t = u_ref[pl.ds(t, rows_c, stride=t_len), :]
        gy_ref[pl.ds(t, rows_c, stride=t_len), :] = jax.nn.gelu(y + d_skip * u_t)

    u_s = u_ref[row_s:row_s + ns, :]
    bu = jnp.dot(u_s.astype(BF16), ws_ref[(t_len - 1) * LANES:, :], preferred_element_type=F32)
    a1_r = abar_r[:, :half]
    a1_i = abar_i[:, :half]
    hre = hre_in_ref[...]
    him = him_in_ref[...]
    nre = a1_r * hre - a1_i * him + bu[:, :half]
    nim = a1_r * him + a1_i * hre + bu[:, half:]
    hre_s_ref[...] = nre
    him_s_ref[...] = nim
    h_cat = jnp.concatenate([nre, nim], axis=1).astype(BF16)
    y_s = _nt_dot(h_cat, c1.astype(BF16))
    gy_ref[row_s:row_s + ns, :] = jax.nn.gelu(y_s + d_skip * u_s)


def _ssm_weights(a_re, a_im, log_dt, b_re, b_im, c_re, c_im, ssm_d):
    nl, g, n = a_re.shape
    p = b_re.shape[3]
    nbd = g // SSM_GB

    def rowvec(a):
        a = a.reshape(nl, nbd, 1, SSM_GB * n)
        return jnp.concatenate([a, a], axis=-1)

    eye = jnp.eye(SSM_GB, dtype=bool)[None, None, :, None, :, None]

    def blockdiag(m):
        e = jnp.where(eye, m[:, :, :, :, None, :], 0.0)
        return e.reshape(nl, nbd, SSM_GB * p, SSM_GB * n)

    bre = blockdiag(b_re.reshape(nl, nbd, SSM_GB, n, p).transpose(0, 1, 2, 4, 3))
    bim = blockdiag(b_im.reshape(nl, nbd, SSM_GB, n, p).transpose(0, 1, 2, 4, 3))
    cre = blockdiag(c_re.reshape(nl, nbd, SSM_GB, p, n))
    cim = blockdiag(c_im.reshape(nl, nbd, SSM_GB, p, n))
    return dict(
        are=rowvec(a_re), aim=rowvec(a_im),
        ldt=rowvec(jnp.broadcast_to(log_dt[:, :, None], (nl, g, n))),
        b1=jnp.concatenate([bre, bim], axis=-1), b2=jnp.concatenate([-bim, bre], axis=-1),
        c1=jnp.concatenate([cre, -cim], axis=-1), c2=jnp.concatenate([-cim, -cre], axis=-1),
        d=ssm_d.reshape(nl, nbd, 1, SSM_GB * p),
    )


def _ssm(s_all, sw, state_re, state_im, l, *, nb, seq, ns):
    rows, ch = s_all.shape
    nbd = ch // LANES
    nc = seq // SSM_T
    rows_c = nb * nc
    half = SSM_SW // 2
    gp = SSM_GB * SSM_P
    row = lambda w: pl.BlockSpec((None, None, 1, w), lambda j: (l, j, 0, 0))
    mat = lambda: pl.BlockSpec((None, None, gp, SSM_SW), lambda j: (l, j, 0, 0))
    kern = functools.partial(_ssm_kernel, nb=nb, nc=nc, ns=ns)
    return pl.pallas_call(
        kern,
        grid=(nbd,),
        in_specs=[
            pl.BlockSpec((rows, LANES), lambda j: (0, j)),
            row(SSM_SW), row(SSM_SW), row(SSM_SW),
            mat(), mat(), mat(), mat(),
            row(gp),
            pl.BlockSpec((None, ns, half), lambda j: (l, 0, j)),
            pl.BlockSpec((None, ns, half), lambda j: (l, 0, j)),
        ],
        out_specs=[
            pl.BlockSpec((rows, LANES), lambda j: (0, j)),
            pl.BlockSpec((nb, half), lambda j: (0, j)),
            pl.BlockSpec((nb, half), lambda j: (0, j)),
            pl.BlockSpec((ns, half), lambda j: (0, j)),
            pl.BlockSpec((ns, half), lambda j: (0, j)),
        ],
        out_shape=[
            jax.ShapeDtypeStruct((rows, ch), F32),
            jax.ShapeDtypeStruct((nb, nbd * half), F32),
            jax.ShapeDtypeStruct((nb, nbd * half), F32),
            jax.ShapeDtypeStruct((ns, nbd * half), F32),
            jax.ShapeDtypeStruct((ns, nbd * half), F32),
        ],
        scratch_shapes=[
            pltpu.VMEM((SSM_T * LANES, SSM_SW), BF16),
            pltpu.VMEM((SSM_T * LANES, SSM_SW), BF16),
            pltpu.VMEM((SSM_T * LANES, LANES), BF16),
            pltpu.VMEM((rows_c, SSM_T * LANES), BF16),
            pltpu.VMEM((rows_c, SSM_SW), F32),
            pltpu.VMEM((rows_c, SSM_SW), F32),
            pltpu.VMEM((rows_c, SSM_SW), BF16),
        ],
        compiler_params=_cparams(("parallel",)),
        name="ssm",
    )(s_all, sw["are"], sw["aim"], sw["ldt"], sw["b1"], sw["b2"], sw["c1"], sw["c2"], sw["d"],
      state_re, state_im)


def _glu_norm_kernel(gy_ref, w_ref, bg_ref, o_ref):
    gy = gy_ref[...]
    z = jnp.dot(gy.astype(BF16), w_ref[...], preferred_element_type=F32)
    o_ref[...] = _rms_rows(gy * jax.nn.sigmoid(z), bg_ref[...]).astype(o_ref.dtype)


def _glu_norm(gy, w_glu, bg, l, *, tm):
    rows, ch = gy.shape
    return pl.pallas_call(
        _glu_norm_kernel,
        grid=(rows // tm,),
        in_specs=[
            pl.BlockSpec((tm, ch), lambda i: (i, 0)),
            pl.BlockSpec((None, ch, ch), lambda i: (l, 0, 0)),
            pl.BlockSpec((None, 1, ch), lambda i: (l, 0, 0)),
        ],
        out_specs=pl.BlockSpec((tm, ch), lambda i: (i, 0)),
        out_shape=jax.ShapeDtypeStruct((rows, ch), BF16),
        compiler_params=_cparams(("parallel",)),
        name="glu_norm",
    )(gy, w_glu, bg)


def _mix_out_kernel(c_ref, s_ref, wc_ref, ws_ref, x_ref, o_ref):
    acc = jnp.dot(c_ref[...], wc_ref[...], preferred_element_type=F32)
    acc = acc + jnp.dot(s_ref[...], ws_ref[...], preferred_element_type=F32)
    o_ref[...] = x_ref[...] + acc


def _mix_out(cn, sn, w_out, x, l, *, tm, tn):
    rows, d = x.shape
    ch = cn.shape[1]
    return pl.pallas_call(
        _mix_out_kernel,
        grid=(rows // tm, d // tn),
        in_specs=[
            pl.BlockSpec((tm, ch), lambda i, j: (i, 0)),
            pl.BlockSpec((tm, ch), lambda i, j: (i, 0)),
            pl.BlockSpec((None, ch, tn), lambda i, j: (l, 0, j)),
            pl.BlockSpec((None, ch, tn), lambda i, j: (l, 1, j)),
            pl.BlockSpec((tm, tn), lambda i, j: (i, j)),
        ],
        out_specs=pl.BlockSpec((tm, tn), lambda i, j: (i, j)),
        out_shape=jax.ShapeDtypeStruct((rows, d), F32),
        compiler_params=_cparams(("parallel", "arbitrary")),
        name="mix_out",
    )(cn, sn, w_out, w_out, x)


def _norm_proj_kernel(x_ref, gain_ref, *refs):
    nw = (len(refs) - 1) // 2
    w_refs, o_refs, hn_ref = refs[:nw], refs[nw:2 * nw], refs[2 * nw]
    _norm_rows_to(x_ref, gain_ref, hn_ref, x_ref.shape[0])
    hn = hn_ref[...]
    for w_ref, o_ref in zip(w_refs, o_refs):
        o_ref[...] = jnp.dot(hn, w_ref[...], preferred_element_type=F32)


def _norm_proj(x, gain, weights, l, *, tm, name):
    rows, d = x.shape
    n = weights[0].shape[2]
    return pl.pallas_call(
        _norm_proj_kernel,
        grid=(rows // tm,),
        in_specs=[pl.BlockSpec((tm, d), lambda i: (i, 0)),
                  pl.BlockSpec((None, 1, d), lambda i: (l, 0, 0))]
                 + [pl.BlockSpec((None, d, n), lambda i: (l, 0, 0)) for _ in weights],
        out_specs=[pl.BlockSpec((tm, n), lambda i: (i, 0)) for _ in weights],
        out_shape=[jax.ShapeDtypeStruct((rows, n), F32) for _ in weights],
        scratch_shapes=[pltpu.VMEM((tm, d), BF16)],
        compiler_params=_cparams(("parallel",)),
        name=name,
    )(x, gain, *weights)


def _proj_res_kernel(a_ref, w_ref, x_ref, o_ref):
    o_ref[...] = x_ref[...] + jnp.dot(a_ref[...].astype(BF16), w_ref[...],
                                      preferred_element_type=F32)


def _proj_res(a, w, x, l, *, tm, tn, name):
    rows, d = x.shape
    k = a.shape[1]
    return pl.pallas_call(
        _proj_res_kernel,
        grid=(rows // tm, d // tn),
        in_specs=[
            pl.BlockSpec((tm, k), lambda i, j: (i, 0)),
            pl.BlockSpec((None, k, tn), lambda i, j: (l, 0, j)),
            pl.BlockSpec((tm, tn), lambda i, j: (i, j)),
        ],
        out_specs=pl.BlockSpec((tm, tn), lambda i, j: (i, j)),
        out_shape=jax.ShapeDtypeStruct((rows, d), F32),
        compiler_params=_cparams(("parallel", "arbitrary")),
        name=name,
    )(a, w, x)


def _softmax_rows(s):
    p = jnp.exp(s - jnp.max(s, axis=-1, keepdims=True))
    return p / jnp.sum(p, axis=-1, keepdims=True)


def _attn_prompt_kernel(q_ref, k_ref, v_ref, o_ref, *, heads, dh):
    scale = dh ** -0.5
    for h in range(heads):
        sl = slice(h * dh, (h + 1) * dh)
        s = _nt_dot(q_ref[:, sl].astype(BF16), k_ref[:, sl].astype(BF16)) * scale
        w = _softmax_rows(s)
        o_ref[:, sl] = jnp.dot(w.astype(BF16), v_ref[:, sl].astype(BF16),
                               preferred_element_type=F32)


def _attn_prompt(q_all, k, v, *, nb, seq, tq, heads):
    rows, width = q_all.shape
    n_mem = k.shape[0] // nb
    per_seq = seq // tq
    kern = functools.partial(_attn_prompt_kernel, heads=heads, dh=width // heads)
    return pl.pallas_call(
        kern,
        grid=(nb, per_seq),
        in_specs=[
            pl.BlockSpec((tq, width), lambda b, i: (b * per_seq + i, 0)),
            pl.BlockSpec((n_mem, width), lambda b, i: (b, 0)),
            pl.BlockSpec((n_mem, width), lambda b, i: (b, 0)),
        ],
        out_specs=pl.BlockSpec((tq, width), lambda b, i: (b * per_seq + i, 0)),
        out_shape=jax.ShapeDtypeStruct((rows, width), F32),
        compiler_params=_cparams(("parallel", "arbitrary")),
        name="attn_prompt",
    )(q_all, k, v)


def _attn_sample_kernel(q_ref, k_ref, v_ref, o_any_ref, o_ref, *, heads, dh):
    del o_any_ref
    bs, width = q_ref.shape
    hp = 8
    q = q_ref[...]
    head_of_lane = lax.broadcasted_iota(jnp.int32, (bs, hp, width), 2) // dh
    row = lax.broadcasted_iota(jnp.int32, (bs, hp, width), 1)
    own = head_of_lane == row
    qe = jnp.where(own, q[:, None, :], 0.0).astype(BF16)
    s = jnp.einsum("bhd,bmd->bhm", qe, k_ref[...].astype(BF16),
                   preferred_element_type=F32) * (dh ** -0.5)
    w = _softmax_rows(s)
    o = jnp.einsum("bhm,bmd->bhd", w.astype(BF16), v_ref[...].astype(BF16),
                   preferred_element_type=F32)
    o_ref[...] = jnp.sum(jnp.where(own, o, 0.0), axis=1)


def _attn_sample(q_all, k_cache, v_cache, o_all, l, *, row0, bs, heads):
    _, ns, n_mem, width = k_cache.shape
    kern = functools.partial(_attn_sample_kernel, heads=heads, dh=width // heads)
    return pl.pallas_call(
        kern,
        grid=(ns // bs,),
        in_specs=[
            pl.BlockSpec((bs, width), lambda j: (row0 // bs + j, 0)),
            pl.BlockSpec((None, bs, n_mem, width), lambda j: (l, j, 0, 0)),
            pl.BlockSpec((None, bs, n_mem, width), lambda j: (l, j, 0, 0)),
            pl.BlockSpec(memory_space=pl.ANY),
        ],
        out_specs=pl.BlockSpec((bs, width), lambda j: (row0 // bs + j, 0)),
        out_shape=jax.ShapeDtypeStruct(o_all.shape, o_all.dtype),
        input_output_aliases={3: 0},
        compiler_params=_cparams(("parallel",)),
        name="attn_sample",
    )(q_all, k_cache, v_cache, o_all)


def _ffn_kernel(x_ref, gain_ref, wu_ref, wd_ref, o_ref, hn_ref):
    @pl.when(pl.program_id(1) == 0)
    def _():
        _norm_rows_to(x_ref, gain_ref, hn_ref, x_ref.shape[0])
        o_ref[...] = x_ref[...]

    h = jnp.dot(hn_ref[...], wu_ref[...], preferred_element_type=F32)
    h = jnp.square(jnp.maximum(h, 0.0)).astype(BF16)
    o_ref[...] += jnp.dot(h, wd_ref[...], preferred_element_type=F32)


def _ffn(x, gain, w_up, w_down, l, *, tm, tf):
    rows, d = x.shape
    dff = w_up.shape[2]
    return pl.pallas_call(
        _ffn_kernel,
        grid=(rows // tm, dff // tf),
        in_specs=[
            pl.BlockSpec((tm, d), lambda i, k: (i, 0)),
            pl.BlockSpec((None, 1, d), lambda i, k: (l, 0, 0)),
            pl.BlockSpec((None, d, tf), lambda i, k: (l, 0, k)),
            pl.BlockSpec((None, tf, d), lambda i, k: (l, k, 0)),
        ],
        out_specs=pl.BlockSpec((tm, d), lambda i, k: (i, 0)),
        out_shape=jax.ShapeDtypeStruct((rows, d), F32),
        scratch_shapes=[pltpu.VMEM((tm, d), BF16)],
        compiler_params=_cparams(("parallel", "arbitrary")),
        name="ffn",
    )(x, gain, w_up, w_down)


def _final_norm_kernel(x_ref, gain_ref, o_ref):
    o_ref[...] = _rms_rows(x_ref[...], gain_ref[...])


def _final_norm(x, gain, *, row0, nrows, tm):
    d = x.shape[1]
    return pl.pallas_call(
        _final_norm_kernel,
        grid=(nrows // tm,),
        in_specs=[pl.BlockSpec((tm, d), lambda i: (row0 // tm + i, 0)),
                  pl.BlockSpec((1, d), lambda i: (0, 0))],
        out_specs=pl.BlockSpec((tm, d), lambda i: (i, 0)),
        out_shape=jax.ShapeDtypeStruct((nrows, d), F32),
        compiler_params=_cparams(("parallel",)),
        name="final_norm",
    )(x, gain)


def kernel(x_prompt, x_sample, mem_prompt, cache_conv, state_ssm_re, state_ssm_im, cache_mem_k, cache_mem_v, norm_mix_g, w_in, conv_w, conv_b, conv_ln_g, conv_ln_b, ssm_a_re, ssm_a_im, ssm_log_dt, ssm_b_re, ssm_b_im, ssm_c_re, ssm_c_im, ssm_d, w_glu, branch_g_conv, branch_g_ssm, w_out, norm_x_g, norm_mem_g, w_xq, w_xk, w_xv, w_xo, norm_ffn_g, w_up, w_down, norm_final_g):
    nb, seq, d = x_prompt.shape
    ns = x_sample.shape[0]
    depth = w_in.shape[0]
    n_mem = mem_prompt.shape[1]
    heads, dh = cache_mem_k.shape[3], cache_mem_k.shape[4]
    width = heads * dh
    ch = conv_w.shape[2]
    g_ssm, n_state = ssm_a_re.shape[1], ssm_a_re.shape[2]
    rows_p = nb * seq
    rows = rows_p + ns

    x = jnp.concatenate([x_prompt.reshape(rows_p, d), x_sample.reshape(ns, d)], axis=0)
    mem = mem_prompt.reshape(nb * n_mem, d)
    vec = lambda a: a.reshape(depth, 1, a.shape[-1])
    mix_g, x_g, mem_g, ffn_g = vec(norm_mix_g), vec(norm_x_g), vec(norm_mem_g), vec(norm_ffn_g)
    cb, lng, lnb, bgc, bgs = (vec(conv_b), vec(conv_ln_g), vec(conv_ln_b),
                              vec(branch_g_conv), vec(branch_g_ssm))
    w_in_b, w_glu_b, w_out_b = w_in.astype(BF16), w_glu.astype(BF16), w_out.astype(BF16)
    w_xq_b, w_xk_b, w_xv_b, w_xo_b = (w_xq.astype(BF16), w_xk.astype(BF16),
                                      w_xv.astype(BF16), w_xo.astype(BF16))
    w_up_b, w_down_b = w_up.astype(BF16), w_down.astype(BF16)
    sw = _ssm_weights(ssm_a_re, ssm_a_im, ssm_log_dt, ssm_b_re, ssm_b_im, ssm_c_re, ssm_c_im, ssm_d)
    st_re = state_ssm_re.reshape(depth, ns, g_ssm * n_state)
    st_im = state_ssm_im.reshape(depth, ns, g_ssm * n_state)
    k_cache = cache_mem_k.reshape(depth, ns, n_mem, width)
    v_cache = cache_mem_v.reshape(depth, ns, n_mem, width)

    tm = _pick_tile(rows, 640, NORM_CHUNK)
    mk_p, mv_p, cb_p, sr_p, si_p, cb_s, sr_s, si_s = [], [], [], [], [], [], [], []
    for l in range(depth):
        g_all, s_all = _in_proj(x, mix_g, w_in_b, l, tm=tm, tn=_pick_tile(ch, 256, LANES))
        cn = _conv_prompt(g_all, conv_w, cb, lng, lnb, bgc, l, nb=nb, seq=seq)
        cache_new, cn = _conv_sample(cache_conv, g_all, conv_w, cb, lng, lnb, bgc, cn, l,
                                     row0=rows_p, bs=16)
        gy, hre_p, him_p, hre_s, him_s = _ssm(s_all, sw, st_re, st_im, l, nb=nb, seq=seq, ns=ns)
        sn = _glu_norm(gy, w_glu_b, bgs, l, tm=_pick_tile(rows, 320, NORM_CHUNK))
        x = _mix_out(cn, sn, w_out_b, x, l, tm=tm, tn=_pick_tile(d, 512, LANES))
        k_p, v_p = _norm_proj(mem, mem_g, [w_xk_b, w_xv_b], l,
                              tm=_pick_tile(nb * n_mem, 512, NORM_CHUNK), name="mem_kv")
        (q_all,) = _norm_proj(x, x_g, [w_xq_b], l, tm=tm, name="q_proj")
        o_all = _attn_prompt(q_all, k_p, v_p, nb=nb, seq=seq, tq=_pick_tile(seq, 512, 8),
                             heads=heads)
        o_all = _attn_sample(q_all, k_cache, v_cache, o_all, l, row0=rows_p, bs=8, heads=heads)
        x = _proj_res(o_all, w_xo_b, x, l, tm=tm, tn=_pick_tile(d, 1024, LANES), name="attn_out")
        x = _ffn(x, ffn_g, w_up_b, w_down_b, l, tm=_pick_tile(rows, 416, NORM_CHUNK),
                 tf=_pick_tile(w_up.shape[2], 512, LANES))

        mk_p.append(k_p.reshape(nb, n_mem, heads, dh))
        mv_p.append(v_p.reshape(nb, n_mem, heads, dh))
        cb_p.append(g_all[:rows_p].reshape(nb, seq, ch)[:, seq - (CONV_W - 1):])
        sr_p.append(hre_p.reshape(nb, g_ssm, n_state))
        si_p.append(him_p.reshape(nb, g_ssm, n_state))
        cb_s.append(cache_new)
        sr_s.append(hre_s.reshape(ns, g_ssm, n_state))
        si_s.append(him_s.reshape(ns, g_ssm, n_state))

    gain = norm_final_g.reshape(1, d)
    y_prompt = _final_norm(x, gain, row0=0, nrows=rows_p, tm=512).reshape(nb, seq, d)
    y_sample = _final_norm(x, gain, row0=rows_p, nrows=ns, tm=ns).reshape(ns, 1, d)
    return (y_prompt, y_sample, jnp.stack(mk_p), jnp.stack(mv_p), jnp.stack(cb_p),
            jnp.stack(sr_p), jnp.stack(si_p), jnp.stack(cb_s), jnp.stack(sr_s), jnp.stack(si_s))
```

```python
import functools

import jax
import jax.numpy as jnp
from jax import lax
from jax.experimental import pallas as pl
from jax.experimental.pallas import tpu as pltpu

F32 = jnp.float32
BF16 = jnp.bfloat16
EPS = 1e-6

V7X_VMEM_LIMIT_BYTES = 56 * 1024 * 1024
LANES = 128
NORM_CHUNK = 32

CONV_W = 31
CONV_HALO = 32
CONV_T = 64

SSM_T = 16
SSM_GB = 8
SSM_P = 16
SSM_N = 64
SSM_SW = 2 * SSM_GB * SSM_N


def _cparams(sem):
    return pltpu.CompilerParams(dimension_semantics=sem,
                                vmem_limit_bytes=V7X_VMEM_LIMIT_BYTES)


def _rms_rows(x, g):
    ms = jnp.mean(x * x, axis=-1, keepdims=True)
    return x * lax.rsqrt(ms + EPS) * g


def _norm_rows_to(x_ref, gain_ref, hn_ref, rows):
    gain = gain_ref[...]

    def body(c, carry):
        r = pl.multiple_of(c * NORM_CHUNK, NORM_CHUNK)
        hn_ref[pl.ds(r, NORM_CHUNK), :] = _rms_rows(
            x_ref[pl.ds(r, NORM_CHUNK), :], gain).astype(hn_ref.dtype)
        return carry

    lax.fori_loop(0, rows // NORM_CHUNK, body, 0)


def _pick_tile(n, target, mult):
    best = None
    for t in range(mult, min(n, target) + 1, mult):
        if n % t == 0:
            best = t
    if best is None:
        raise ValueError(f"no tile for extent {n} (multiple of {mult}, <= {target})")
    return best


def _nt_dot(a, b):
    return lax.dot_general(a, b, (((1,), (1,)), ((), ())),
                           preferred_element_type=F32)


def _in_proj_kernel(x_ref, gain_ref, wv_ref, wg_ref, ws_ref, g_ref, s_ref, hn_ref):
    @pl.when(pl.program_id(1) == 0)
    def _():
        _norm_rows_to(x_ref, gain_ref, hn_ref, x_ref.shape[0])

    hn = hn_ref[...]
    val = jnp.dot(hn, wv_ref[...], preferred_element_type=F32)
    gate = jnp.dot(hn, wg_ref[...], preferred_element_type=F32)
    g_ref[...] = val * jax.nn.sigmoid(gate)
    s_ref[...] = jnp.dot(hn, ws_ref[...], preferred_element_type=F32)


def _in_proj(x, gain, w_in, l, *, tm, tn):
    rows, d = x.shape
    ch = w_in.shape[2] // 3
    nj = ch // tn
    return pl.pallas_call(
        _in_proj_kernel,
        grid=(rows // tm, nj),
        in_specs=[
            pl.BlockSpec((tm, d), lambda i, j: (i, 0)),
            pl.BlockSpec((None, 1, d), lambda i, j: (l, 0, 0)),
            pl.BlockSpec((None, d, tn), lambda i, j: (l, 0, j)),
            pl.BlockSpec((None, d, tn), lambda i, j: (l, 0, nj + j)),
            pl.BlockSpec((None, d, tn), lambda i, j: (l, 0, 2 * nj + j)),
        ],
        out_specs=[
            pl.BlockSpec((tm, tn), lambda i, j: (i, j)),
            pl.BlockSpec((tm, tn), lambda i, j: (i, j)),
        ],
        out_shape=[jax.ShapeDtypeStruct((rows, ch), F32),
                   jax.ShapeDtypeStruct((rows, ch), F32)],
        scratch_shapes=[pltpu.VMEM((tm, d), BF16)],
        compiler_params=_cparams(("parallel", "arbitrary")),
        name="in_proj",
    )(x, gain, w_in, w_in, w_in)


def _conv_tail(c, b_ref, lng_ref, lnb_ref, bg_ref):
    c = c + b_ref[...]
    xc = c - jnp.mean(c, axis=-1, keepdims=True)
    var = jnp.mean(xc * xc, axis=-1, keepdims=True)
    y = xc * lax.rsqrt(var + EPS) * lng_ref[...] + lnb_ref[...]
    y = y * jax.nn.sigmoid(y)
    return _rms_rows(y, bg_ref[...])


def _conv_prompt_kernel(cur_ref, prev_ref, wb_ref, b_ref, lng_ref, lnb_ref, bg_ref,
                        o_ref, ext_ref, c_ref):
    t, ch = cur_ref.shape
    rows_e = CONV_HALO + t
    first = pl.program_id(1) == 0
    ext_ref[0, 0:CONV_HALO, :] = jnp.where(first, 0.0, prev_ref[...])
    ext_ref[0, CONV_HALO:rows_e, :] = cur_ref[...]
    for r in range(1, 8):
        ext_ref[r] = pltpu.roll(ext_ref[0], rows_e - r, axis=0)
    off = CONV_HALO - (CONV_W - 1)
    n_rb = t // 8
    for lc in range(ch // LANES):
        lanes = slice(lc * LANES, (lc + 1) * LANES)
        w = [wb_ref[k, :, lanes] for k in range(CONV_W)]
        acc = [None] * n_rb
        for r in range(8):
            taps = [(k, (off + k) // 8) for k in range(CONV_W) if (off + k) % 8 == r]
            for m in range(rows_e // 8):
                uses = [(k, m - a) for k, a in taps if 0 <= m - a < n_rb]
                if not uses:
                    continue
                v = ext_ref[r, m * 8:(m + 1) * 8, lanes]
                for k, rb in uses:
                    term = w[k] * v
                    acc[rb] = term if acc[rb] is None else acc[rb] + term
        for rb in range(n_rb):
            c_ref[rb * 8:(rb + 1) * 8, lanes] = acc[rb]
    o_ref[...] = _conv_tail(c_ref[...], b_ref, lng_ref, lnb_ref, bg_ref).astype(o_ref.dtype)


def _conv_prompt(g_all, conv_wb, conv_b, ln_g, ln_b, bg, l, *, nb, seq):
    rows, ch = g_all.shape
    t = CONV_T
    per_seq = seq // t
    vec = lambda: pl.BlockSpec((None, 1, ch), lambda b, i: (l, 0, 0))
    return pl.pallas_call(
        _conv_prompt_kernel,
        grid=(nb, per_seq),
        in_specs=[
            pl.BlockSpec((t, ch), lambda b, i: (b * per_seq + i, 0)),
            pl.BlockSpec((CONV_HALO, ch),
                         lambda b, i: (jnp.maximum((b * seq + i * t) // CONV_HALO - 1, 0), 0)),
            pl.BlockSpec((None, CONV_W, 8, ch), lambda b, i: (l, 0, 0, 0)),
            vec(), vec(), vec(), vec(),
        ],
        out_specs=pl.BlockSpec((t, ch), lambda b, i: (b * per_seq + i, 0)),
        out_shape=jax.ShapeDtypeStruct((rows, ch), BF16),
        scratch_shapes=[pltpu.VMEM((8, CONV_HALO + t, ch), F32), pltpu.VMEM((t, ch), F32)],
        compiler_params=_cparams(("parallel", "arbitrary")),
        name="conv_prompt",
    )(g_all, g_all, conv_wb, conv_b, ln_g, ln_b, bg)


def _conv_sample_kernel(cache_ref, g_ref, w_ref, b_ref, lng_ref, lnb_ref, bg_ref, *refs):
    cache_out_ref, cn_ref = refs[-2:]
    hist = CONV_W - 1
    g = g_ref[...]
    c = w_ref[hist:CONV_W, :] * g
    for k in range(hist):
        c = c + w_ref[k:k + 1, :] * cache_ref[k]
    cn_ref[...] = _conv_tail(c, b_ref, lng_ref, lnb_ref, bg_ref).astype(cn_ref.dtype)
    cache_out_ref[0:hist - 1] = cache_ref[1:hist]
    cache_out_ref[hist - 1] = g


def _conv_sample(cache_t, g_all, conv_w, conv_b, ln_g, ln_b, bg, cn_all, cache_out, l, *, row0, bs):
    _, hist, ns, ch = cache_t.shape
    vec = lambda: pl.BlockSpec((None, 1, ch), lambda j: (l, 0, 0))
    aliased = [cn_all] if cache_out is None else [cn_all, cache_out]
    return pl.pallas_call(
        _conv_sample_kernel,
        grid=(ns // bs,),
        in_specs=[
            pl.BlockSpec((None, hist, bs, ch), lambda j: (l, 0, j, 0)),
            pl.BlockSpec((bs, ch), lambda j: (row0 // bs + j, 0)),
            pl.BlockSpec((None, CONV_W, ch), lambda j: (l, 0, 0)),
            vec(), vec(), vec(), vec(),
        ] + [pl.BlockSpec(memory_space=pl.ANY) for _ in aliased],
        out_specs=[
            pl.BlockSpec((None, hist, bs, ch), lambda j: (l, 0, j, 0)),
            pl.BlockSpec((bs, ch), lambda j: (row0 // bs + j, 0)),
        ],
        out_shape=[jax.ShapeDtypeStruct(cache_t.shape, F32),
                   jax.ShapeDtypeStruct(cn_all.shape, cn_all.dtype)],
        input_output_aliases={7: 1} if cache_out is None else {7: 1, 8: 0},
        compiler_params=_cparams(("parallel",)),
        name="conv_sample",
    )(cache_t, g_all, conv_w, conv_b, ln_g, ln_b, bg, *aliased)


def _ssm_kernel(u_ref, are_ref, aim_ref, ldt_ref, b1_ref, b2_ref, c1_ref, c2_ref, d_ref,
                hre_in_ref, him_in_ref,
                gy_ref, hre_p_ref, him_p_ref, hre_s_ref, him_s_ref,
                ws_ref, wit_ref, dst_ref, x_ref, s_ref, hp_ref, hpb_ref, *, nb, nc, ns):
    t_len = SSM_T
    half = SSM_SW // 2
    rows_c = nb * nc
    row_s = rows_c * t_len

    a_re = are_ref[...]
    a_im = aim_ref[...]
    dt = jnp.exp(ldt_ref[...])
    mag = jnp.exp(a_re * dt)
    ang = a_im * dt
    abar_r = mag * jnp.cos(ang)
    abar_i = mag * jnp.sin(ang)
    nr = abar_r - 1.0
    ni = abar_i
    den = a_re * a_re + a_im * a_im
    cf_r = (nr * a_re + ni * a_im) / den
    cf_i = (ni * a_re - nr * a_im) / den

    pw = [(jnp.ones_like(abar_r), jnp.zeros_like(abar_r))]
    for _ in range(t_len):
        pr, pi = pw[-1]
        pw.append((pr * abar_r - pi * abar_i, pr * abar_i + pi * abar_r))

    b1 = b1_ref[...]
    b2 = b2_ref[...]
    c1 = c1_ref[...]
    c2 = c2_ref[...]
    for t in range(t_len):
        sl = slice(t * LANES, (t + 1) * LANES)
        pr, pi = pw[t_len - 1 - t]
        acr = pr * cf_r - pi * cf_i
        aci = pr * cf_i + pi * cf_r
        ws_t = acr * b1 + aci * b2
        ws_ref[sl, :] = ws_t.astype(BF16)
        dst_ref[sl, :] = lax.dot_general(
            ws_t, c1, (((1,), (1,)), ((), ())), precision=lax.Precision.HIGHEST,
            preferred_element_type=F32).astype(BF16)
        pr, pi = pw[t + 1]
        wit_ref[sl, :] = (pr * c1 + pi * c2).astype(BF16)
        x_ref[:, sl] = u_ref[pl.ds(t, rows_c, stride=t_len), :].astype(BF16)

    s_ref[...] = jnp.dot(x_ref[...], ws_ref[...], preferred_element_type=F32)
    at_r = pw[t_len][0][:, :half]
    at_i = pw[t_len][1][:, :half]

    def scan_body(c, carry):
        new = []
        for b in range(nb):
            hr, hi = carry[b]
            row = pl.ds(b * nc + c, 1)
            hp_ref[row, 0:half] = hr
            hp_ref[row, half:] = hi
            sr = s_ref[row, 0:half]
            si = s_ref[row, half:]
            new.append((at_r * hr - at_i * hi + sr, at_r * hi + at_i * hr + si))
        return tuple(new)

    zero = jnp.zeros((1, half), F32)
    final = lax.fori_loop(0, nc, scan_body, tuple((zero, zero) for _ in range(nb)))
    for b in range(nb):
        hre_p_ref[b:b + 1, :] = final[b][0]
        him_p_ref[b:b + 1, :] = final[b][1]
    hpb_ref[...] = hp_ref[...].astype(BF16)

    d_skip = d_ref[...]
    for t in range(t_len):
        sl = slice(t * LANES, (t + 1) * LANES)
        y = jnp.dot(x_ref[:, 0:(t + 1) * LANES], dst_ref[(t_len - 1 - t) * LANES:, :],
                    preferred_element_type=F32)
        y = y + _nt_dot(hpb_ref[...], wit_ref[sl, :])
        u_t = u_ref[pl.ds(t, rows_c, stride=t_len), :]
        gy_ref[pl.ds(t, rows_c, stride=t_len), :] = jax.nn.gelu(y + d_skip * u_t)

    u_s = u_ref[row_s:row_s + ns, :]
    bu = jnp.dot(u_s.astype(BF16), ws_ref[(t_len - 1) * LANES:, :], preferred_element_type=F32)
    a1_r = abar_r[:, :half]
    a1_i = abar_i[:, :half]
    hre = hre_in_ref[...]
    him = him_in_ref[...]
    nre = a1_r * hre - a1_i * him + bu[:, :half]
    nim = a1_r * him + a1_i * hre + bu[:, half:]
    hre_s_ref[...] = nre
    him_s_ref[...] = nim
    h_cat = jnp.concatenate([nre, nim], axis=1).astype(BF16)
    y_s = _nt_dot(h_cat, c1.astype(BF16))
    gy_ref[row_s:row_s + ns, :] = jax.nn.gelu(y_s + d_skip * u_s)


def _ssm_weights(a_re, a_im, log_dt, b_re, b_im, c_re, c_im, ssm_d):
    nl, g, n = a_re.shape
    p = b_re.shape[3]
    nbd = g // SSM_GB

    def rowvec(a):
        a = a.reshape(nl, nbd, 1, SSM_GB * n)
        return jnp.concatenate([a, a], axis=-1)

    eye = jnp.eye(SSM_GB, dtype=bool)[None, None, :, None, :, None]

    def blockdiag(m):
        e = jnp.where(eye, m[:, :, :, :, None, :], 0.0)
        return e.reshape(nl, nbd, SSM_GB * p, SSM_GB * n)

    bre = blockdiag(b_re.reshape(nl, nbd, SSM_GB, n, p).transpose(0, 1, 2, 4, 3))
    bim = blockdiag(b_im.reshape(nl, nbd, SSM_GB, n, p).transpose(0, 1, 2, 4, 3))
    cre = blockdiag(c_re.reshape(nl, nbd, SSM_GB, p, n))
    cim = blockdiag(c_im.reshape(nl, nbd, SSM_GB, p, n))
    return dict(
        are=rowvec(a_re), aim=rowvec(a_im),
        ldt=rowvec(jnp.broadcast_to(log_dt[:, :, None], (nl, g, n))),
        b1=jnp.concatenate([bre, bim], axis=-1), b2=jnp.concatenate([-bim, bre], axis=-1),
        c1=jnp.concatenate([cre, -cim], axis=-1), c2=jnp.concatenate([-cim, -cre], axis=-1),
        d=ssm_d.reshape(nl, nbd, 1, SSM_GB * p),
    )


def _ssm(s_all, sw, state_re, state_im, l, *, nb, seq, ns):
    rows, ch = s_all.shape
    nbd = ch // LANES
    nc = seq // SSM_T
    rows_c = nb * nc
    half = SSM_SW // 2
    gp = SSM_GB * SSM_P
    row = lambda w: pl.BlockSpec((None, None, 1, w), lambda j: (l, j, 0, 0))
    mat = lambda: pl.BlockSpec((None, None, gp, SSM_SW), lambda j: (l, j, 0, 0))
    kern = functools.partial(_ssm_kernel, nb=nb, nc=nc, ns=ns)
    return pl.pallas_call(
        kern,
        grid=(nbd,),
        in_specs=[
            pl.BlockSpec((rows, LANES), lambda j: (0, j)),
            row(SSM_SW), row(SSM_SW), row(SSM_SW),
            mat(), mat(), mat(), mat(),
            row(gp),
            pl.BlockSpec((None, ns, half), lambda j: (l, 0, j)),
            pl.BlockSpec((None, ns, half), lambda j: (l, 0, j)),
        ],
        out_specs=[
            pl.BlockSpec((rows, LANES), lambda j: (0, j)),
            pl.BlockSpec((nb, half), lambda j: (0, j)),
            pl.BlockSpec((nb, half), lambda j: (0, j)),
            pl.BlockSpec((ns, half), lambda j: (0, j)),
            pl.BlockSpec((ns, half), lambda j: (0, j)),
        ],
        out_shape=[
            jax.ShapeDtypeStruct((rows, ch), F32),
            jax.ShapeDtypeStruct((nb, nbd * half), F32),
            jax.ShapeDtypeStruct((nb, nbd * half), F32),
            jax.ShapeDtypeStruct((ns, nbd * half), F32),
            jax.ShapeDtypeStruct((ns, nbd * half), F32),
        ],
        scratch_shapes=[
            pltpu.VMEM((SSM_T * LANES, SSM_SW), BF16),
            pltpu.VMEM((SSM_T * LANES, SSM_SW), BF16),
            pltpu.VMEM((SSM_T * LANES, LANES), BF16),
            pltpu.VMEM((rows_c, SSM_T * LANES), BF16),
            pltpu.VMEM((rows_c, SSM_SW), F32),
            pltpu.VMEM((rows_c, SSM_SW), F32),
            pltpu.VMEM((rows_c, SSM_SW), BF16),
        ],
        compiler_params=_cparams(("parallel",)),
        name="ssm",
    )(s_all, sw["are"], sw["aim"], sw["ldt"], sw["b1"], sw["b2"], sw["c1"], sw["c2"], sw["d"],
      state_re, state_im)


def _glu_norm_kernel(gy_ref, w_ref, bg_ref, o_ref):
    gy = gy_ref[...]
    z = jnp.dot(gy.astype(BF16), w_ref[...], preferred_element_type=F32)
    o_ref[...] = _rms_rows(gy * jax.nn.sigmoid(z), bg_ref[...]).astype(o_ref.dtype)


def _glu_norm(gy, w_glu, bg, l, *, tm):
    rows, ch = gy.shape
    return pl.pallas_call(
        _glu_norm_kernel,
        grid=(rows // tm,),
        in_specs=[
            pl.BlockSpec((tm, ch), lambda i: (i, 0)),
            pl.BlockSpec((None, ch, ch), lambda i: (l, 0, 0)),
            pl.BlockSpec((None, 1, ch), lambda i: (l, 0, 0)),
        ],
        out_specs=pl.BlockSpec((tm, ch), lambda i: (i, 0)),
        out_shape=jax.ShapeDtypeStruct((rows, ch), BF16),
        compiler_params=_cparams(("parallel",)),
        name="glu_norm",
    )(gy, w_glu, bg)


def _mix_out_kernel(c_ref, s_ref, wc_ref, ws_ref, x_ref, o_ref):
    acc = jnp.dot(c_ref[...], wc_ref[...], preferred_element_type=F32)
    acc = acc + jnp.dot(s_ref[...], ws_ref[...], preferred_element_type=F32)
    o_ref[...] = x_ref[...] + acc


def _mix_out(cn, sn, w_out, x, l, *, tm, tn):
    rows, d = x.shape
    ch = cn.shape[1]
    return pl.pallas_call(
        _mix_out_kernel,
        grid=(rows // tm, d // tn),
        in_specs=[
            pl.BlockSpec((tm, ch), lambda i, j: (i, 0)),
            pl.BlockSpec((tm, ch), lambda i, j: (i, 0)),
            pl.BlockSpec((None, ch, tn), lambda i, j: (l, 0, j)),
            pl.BlockSpec((None, ch, tn), lambda i, j: (l, 1, j)),
            pl.BlockSpec((tm, tn), lambda i, j: (i, j)),
        ],
        out_specs=pl.BlockSpec((tm, tn), lambda i, j: (i, j)),
        out_shape=jax.ShapeDtypeStruct((rows, d), F32),
        compiler_params=_cparams(("parallel", "arbitrary")),
        name="mix_out",
    )(cn, sn, w_out, w_out, x)


def _norm_proj_kernel(x_ref, gain_ref, *refs):
    nw = (len(refs) - 1) // 2
    w_refs, o_refs, hn_ref = refs[:nw], refs[nw:2 * nw], refs[2 * nw]
    _norm_rows_to(x_ref, gain_ref, hn_ref, x_ref.shape[0])
    hn = hn_ref[...]
    for w_ref, o_ref in zip(w_refs, o_refs):
        o_ref[...] = jnp.dot(hn, w_ref[...], preferred_element_type=F32)


def _norm_proj(x, gain, weights, l, *, tm, name):
    rows, d = x.shape
    n = weights[0].shape[2]
    return pl.pallas_call(
        _norm_proj_kernel,
        grid=(rows // tm,),
        in_specs=[pl.BlockSpec((tm, d), lambda i: (i, 0)),
                  pl.BlockSpec((None, 1, d), lambda i: (l, 0, 0))]
                 + [pl.BlockSpec((None, d, n), lambda i: (l, 0, 0)) for _ in weights],
        out_specs=[pl.BlockSpec((tm, n), lambda i: (i, 0)) for _ in weights],
        out_shape=[jax.ShapeDtypeStruct((rows, n), F32) for _ in weights],
        scratch_shapes=[pltpu.VMEM((tm, d), BF16)],
        compiler_params=_cparams(("parallel",)),
        name=name,
    )(x, gain, *weights)


def _proj_res_kernel(a_ref, w_ref, x_ref, o_ref):
    o_ref[...] = x_ref[...] + jnp.dot(a_ref[...].astype(BF16), w_ref[...],
                                      preferred_element_type=F32)


def _proj_res(a, w, x, l, *, tm, tn, name):
    rows, d = x.shape
    k = a.shape[1]
    return pl.pallas_call(
        _proj_res_kernel,
        grid=(rows // tm, d // tn),
        in_specs=[
            pl.BlockSpec((tm, k), lambda i, j: (i, 0)),
            pl.BlockSpec((None, k, tn), lambda i, j: (l, 0, j)),
            pl.BlockSpec((tm, tn), lambda i, j: (i, j)),
        ],
        out_specs=pl.BlockSpec((tm, tn), lambda i, j: (i, j)),
        out_shape=jax.ShapeDtypeStruct((rows, d), F32),
        compiler_params=_cparams(("parallel", "arbitrary")),
        name=name,
    )(a, w, x)


def _softmax_rows(s):
    p = jnp.exp(s - jnp.max(s, axis=-1, keepdims=True))
    return p / jnp.sum(p, axis=-1, keepdims=True)


def _attn_prompt_kernel(q_ref, k_ref, v_ref, o_ref, *, heads, dh):
    scale = dh ** -0.5
    for h in range(heads):
        sl = slice(h * dh, (h + 1) * dh)
        s = _nt_dot(q_ref[:, sl].astype(BF16), k_ref[:, sl].astype(BF16)) * scale
        w = _softmax_rows(s)
        o_ref[:, sl] = jnp.dot(w.astype(BF16), v_ref[:, sl].astype(BF16),
                               preferred_element_type=F32)


def _attn_prompt(q_all, k, v, *, nb, seq, tq, heads):
    rows, width = q_all.shape
    n_mem = k.shape[0] // nb
    per_seq = seq // tq
    kern = functools.partial(_attn_prompt_kernel, heads=heads, dh=width // heads)
    return pl.pallas_call(
        kern,
        grid=(nb, per_seq),
        in_specs=[
            pl.BlockSpec((tq, width), lambda b, i: (b * per_seq + i, 0)),
            pl.BlockSpec((n_mem, width), lambda b, i: (b, 0)),
            pl.BlockSpec((n_mem, width), lambda b, i: (b, 0)),
        ],
        out_specs=pl.BlockSpec((tq, width), lambda b, i: (b * per_seq + i, 0)),
        out_shape=jax.ShapeDtypeStruct((rows, width), F32),
        compiler_params=_cparams(("parallel", "arbitrary")),
        name="attn_prompt",
    )(q_all, k, v)


def _attn_sample_kernel(q_ref, k_ref, v_ref, o_any_ref, o_ref, *, heads, dh):
    del o_any_ref
    bs = q_ref.shape[0]
    nr = k_ref.shape[1]
    hp = 8
    q = q_ref[...]
    head_row = lax.broadcasted_iota(jnp.int32, (bs, hp, dh), 1)
    q4 = jnp.zeros((bs, hp, dh), F32)
    for h in range(heads):
        q4 = jnp.where(head_row == h, q[:, None, h * dh:(h + 1) * dh], q4)
    s = jnp.einsum("bhd,brd->bhr", q4.astype(BF16), k_ref[...].astype(BF16),
                   preferred_element_type=F32) * (dh ** -0.5)
    own = (lax.broadcasted_iota(jnp.int32, (bs, hp, nr), 2) % heads
           == lax.broadcasted_iota(jnp.int32, (bs, hp, nr), 1))
    w = _softmax_rows(jnp.where(own, s, -1e30))
    o = jnp.einsum("bhr,brd->bhd", w.astype(BF16), v_ref[...].astype(BF16),
                   preferred_element_type=F32)
    for h in range(heads):
        o_ref[:, h * dh:(h + 1) * dh] = o[:, h, :]


def _attn_sample(q_all, k_cache, v_cache, o_all, l, *, row0, bs, heads):
    _, ns, nr, dh = k_cache.shape
    width = heads * dh
    kern = functools.partial(_attn_sample_kernel, heads=heads, dh=dh)
    return pl.pallas_call(
        kern,
        grid=(ns // bs,),
        in_specs=[
            pl.BlockSpec((bs, width), lambda j: (row0 // bs + j, 0)),
            pl.BlockSpec((None, bs, nr, dh), lambda j: (l, j, 0, 0)),
            pl.BlockSpec((None, bs, nr, dh), lambda j: (l, j, 0, 0)),
            pl.BlockSpec(memory_space=pl.ANY),
        ],
        out_specs=pl.BlockSpec((bs, width), lambda j: (row0 // bs + j, 0)),
        out_shape=jax.ShapeDtypeStruct(o_all.shape, o_all.dtype),
        input_output_aliases={3: 0},
        compiler_params=_cparams(("parallel",)),
        name="attn_sample",
    )(q_all, k_cache, v_cache, o_all)


def _ffn_kernel(x_ref, gain_ref, wu_ref, wd_ref, o_ref, hn_ref):
    @pl.when(pl.program_id(1) == 0)
    def _():
        _norm_rows_to(x_ref, gain_ref, hn_ref, x_ref.shape[0])
        o_ref[...] = x_ref[...]

    h = jnp.dot(hn_ref[...], wu_ref[...], preferred_element_type=F32)
    h = jnp.square(jnp.maximum(h, 0.0)).astype(BF16)
    o_ref[...] += jnp.dot(h, wd_ref[...], preferred_element_type=F32)


def _ffn(x, gain, w_up, w_down, l, *, tm, tf):
    rows, d = x.shape
    dff = w_up.shape[2]
    return pl.pallas_call(
        _ffn_kernel,
        grid=(rows // tm, dff // tf),
        in_specs=[
            pl.BlockSpec((tm, d), lambda i, k: (i, 0)),
            pl.BlockSpec((None, 1, d), lambda i, k: (l, 0, 0)),
            pl.BlockSpec((None, d, tf), lambda i, k: (l, 0, k)),
            pl.BlockSpec((None, tf, d), lambda i, k: (l, k, 0)),
        ],
        out_specs=pl.BlockSpec((tm, d), lambda i, k: (i, 0)),
        out_shape=jax.ShapeDtypeStruct((rows, d), F32),
        scratch_shapes=[pltpu.VMEM((tm, d), BF16)],
        compiler_params=_cparams(("parallel", "arbitrary")),
        name="ffn",
    )(x, gain, w_up, w_down)


def _final_norm_kernel(x_ref, gain_ref, o_ref):
    o_ref[...] = _rms_rows(x_ref[...], gain_ref[...])


def _final_norm(x, gain, *, row0, nrows, tm):
    d = x.shape[1]
    return pl.pallas_call(
        _final_norm_kernel,
        grid=(nrows // tm,),
        in_specs=[pl.BlockSpec((tm, d), lambda i: (row0 // tm + i, 0)),
                  pl.BlockSpec((1, d), lambda i: (0, 0))],
        out_specs=pl.BlockSpec((tm, d), lambda i: (i, 0)),
        out_shape=jax.ShapeDtypeStruct((nrows, d), F32),
        compiler_params=_cparams(("parallel",)),
        name="final_norm",
    )(x, gain)


def kernel(x_prompt, x_sample, mem_prompt, cache_conv, state_ssm_re, state_ssm_im, cache_mem_k, cache_mem_v, norm_mix_g, w_in, conv_w, conv_b, conv_ln_g, conv_ln_b, ssm_a_re, ssm_a_im, ssm_log_dt, ssm_b_re, ssm_b_im, ssm_c_re, ssm_c_im, ssm_d, w_glu, branch_g_conv, branch_g_ssm, w_out, norm_x_g, norm_mem_g, w_xq, w_xk, w_xv, w_xo, norm_ffn_g, w_up, w_down, norm_final_g):
    nb, seq, d = x_prompt.shape
    ns = x_sample.shape[0]
    depth = w_in.shape[0]
    n_mem = mem_prompt.shape[1]
    heads, dh = cache_mem_k.shape[3], cache_mem_k.shape[4]
    width = heads * dh
    ch = conv_w.shape[2]
    g_ssm, n_state = ssm_a_re.shape[1], ssm_a_re.shape[2]
    rows_p = nb * seq
    rows = rows_p + ns

    x = jnp.concatenate([x_prompt.reshape(rows_p, d), x_sample.reshape(ns, d)], axis=0)
    mem = mem_prompt.reshape(nb * n_mem, d)
    vec = lambda a: a.reshape(depth, 1, a.shape[-1])
    mix_g, x_g, mem_g, ffn_g = vec(norm_mix_g), vec(norm_x_g), vec(norm_mem_g), vec(norm_ffn_g)
    cb, lng, lnb, bgc, bgs = (vec(conv_b), vec(conv_ln_g), vec(conv_ln_b),
                              vec(branch_g_conv), vec(branch_g_ssm))
    w_in_b, w_glu_b, w_out_b = w_in.astype(BF16), w_glu.astype(BF16), w_out.astype(BF16)
    w_xq_b, w_xk_b, w_xv_b, w_xo_b = (w_xq.astype(BF16), w_xk.astype(BF16),
                                      w_xv.astype(BF16), w_xo.astype(BF16))
    w_up_b, w_down_b = w_up.astype(BF16), w_down.astype(BF16)
    sw = _ssm_weights(ssm_a_re, ssm_a_im, ssm_log_dt, ssm_b_re, ssm_b_im, ssm_c_re, ssm_c_im, ssm_d)
    st_re = state_ssm_re.reshape(depth, ns, g_ssm * n_state)
    st_im = state_ssm_im.reshape(depth, ns, g_ssm * n_state)
    k_cache = cache_mem_k.reshape(depth, ns, n_mem * heads, dh)
    v_cache = cache_mem_v.reshape(depth, ns, n_mem * heads, dh)
    cache_t = jnp.transpose(cache_conv, (0, 2, 1, 3))

    conv_wb = jnp.broadcast_to(conv_w[:, :, None, :], (depth, CONV_W, 8, ch))
    tm = _pick_tile(rows, 640, NORM_CHUNK)
    cache_out = None
    mk_p, mv_p, cb_p, sr_p, si_p, sr_s, si_s = [], [], [], [], [], [], []
    for l in range(depth):
        g_all, s_all = _in_proj(x, mix_g, w_in_b, l, tm=tm, tn=_pick_tile(ch, 256, LANES))
        cn = _conv_prompt(g_all, conv_wb, cb, lng, lnb, bgc, l, nb=nb, seq=seq)
        cache_out, cn = _conv_sample(cache_t, g_all, conv_w, cb, lng, lnb, bgc, cn, cache_out, l,
                                     row0=rows_p, bs=16)
        gy, hre_p, him_p, hre_s, him_s = _ssm(s_all, sw, st_re, st_im, l, nb=nb, seq=seq, ns=ns)
        sn = _glu_norm(gy, w_glu_b, bgs, l, tm=_pick_tile(rows, 320, NORM_CHUNK))
        x = _mix_out(cn, sn, w_out_b, x, l, tm=tm, tn=_pick_tile(d, 512, LANES))
        k_p, v_p = _norm_proj(mem, mem_g, [w_xk_b, w_xv_b], l,
                              tm=_pick_tile(nb * n_mem, 512, NORM_CHUNK), name="mem_kv")
        (q_all,) = _norm_proj(x, x_g, [w_xq_b], l, tm=tm, name="q_proj")
        o_all = _attn_prompt(q_all, k_p, v_p, nb=nb, seq=seq, tq=_pick_tile(seq, 512, 8),
                             heads=heads)
        o_all = _attn_sample(q_all, k_cache, v_cache, o_all, l, row0=rows_p, bs=8, heads=heads)
        x = _proj_res(o_all, w_xo_b, x, l, tm=tm, tn=_pick_tile(d, 1024, LANES), name="attn_out")
        x = _ffn(x, ffn_g, w_up_b, w_down_b, l, tm=_pick_tile(rows, 416, NORM_CHUNK),
                 tf=_pick_tile(w_up.shape[2], 512, LANES))

        mk_p.append(k_p.reshape(nb, n_mem, heads, dh))
        mv_p.append(v_p.reshape(nb, n_mem, heads, dh))
        cb_p.append(jnp.stack([g_all[(b + 1) * seq - (CONV_W - 1):(b + 1) * seq]
                               for b in range(nb)]))
        sr_p.append(hre_p.reshape(nb, g_ssm, n_state))
        si_p.append(him_p.reshape(nb, g_ssm, n_state))
        sr_s.append(hre_s.reshape(ns, g_ssm, n_state))
        si_s.append(him_s.reshape(ns, g_ssm, n_state))

    gain = norm_final_g.reshape(1, d)
    y_prompt = _final_norm(x, gain, row0=0, nrows=rows_p, tm=512).reshape(nb, seq, d)
    y_sample = _final_norm(x, gain, row0=rows_p, nrows=ns, tm=ns).reshape(ns, 1, d)
    return (y_prompt, y_sample, jnp.stack(mk_p), jnp.stack(mv_p), jnp.stack(cb_p),
            jnp.stack(sr_p), jnp.stack(si_p), jnp.transpose(cache_out, (0, 2, 1, 3)),
            jnp.stack(sr_s), jnp.stack(si_s))
```

```python
import functools

import jax
import jax.numpy as jnp
from jax import lax
from jax.experimental import pallas as pl
from jax.experimental.pallas import tpu as pltpu

F32 = jnp.float32
BF16 = jnp.bfloat16
EPS = 1e-6

V7X_VMEM_LIMIT_BYTES = 56 * 1024 * 1024
V7X_FFN_VMEM_LIMIT_BYTES = 60 * 1024 * 1024
LANES = 128
NORM_CHUNK = 32

CONV_W = 31
CONV_HALO = 32
CONV_T = 64

SSM_T = 16
SSM_GB = 8
SSM_P = 16
SSM_N = 64
SSM_SW = 2 * SSM_GB * SSM_N


def _cparams(sem):
    return pltpu.CompilerParams(dimension_semantics=sem,
                                vmem_limit_bytes=V7X_VMEM_LIMIT_BYTES)


def _rms_rows(x, g):
    ms = jnp.mean(x * x, axis=-1, keepdims=True)
    return x * lax.rsqrt(ms + EPS) * g


def _norm_rows_to(x_ref, gain_ref, hn_ref, rows):
    gain = gain_ref[...]

    def body(c, carry):
        r = pl.multiple_of(c * NORM_CHUNK, NORM_CHUNK)
        hn_ref[pl.ds(r, NORM_CHUNK), :] = _rms_rows(
            x_ref[pl.ds(r, NORM_CHUNK), :], gain).astype(hn_ref.dtype)
        return carry

    lax.fori_loop(0, rows // NORM_CHUNK, body, 0)


def _pick_tile(n, target, mult):
    best = None
    for t in range(mult, min(n, target) + 1, mult):
        if n % t == 0:
            best = t
    if best is None:
        raise ValueError(f"no tile for extent {n} (multiple of {mult}, <= {target})")
    return best


def _nt_dot(a, b):
    return lax.dot_general(a, b, (((1,), (1,)), ((), ())),
                           preferred_element_type=F32)


def _in_proj_kernel(x_ref, gain_ref, wv_ref, wg_ref, ws_ref, g_ref, s_ref, hn_ref):
    @pl.when(pl.program_id(1) == 0)
    def _():
        _norm_rows_to(x_ref, gain_ref, hn_ref, x_ref.shape[0])

    hn = hn_ref[...]
    val = jnp.dot(hn, wv_ref[...], preferred_element_type=F32)
    gate = jnp.dot(hn, wg_ref[...], preferred_element_type=F32)
    g_ref[...] = val * jax.nn.sigmoid(gate)
    s_ref[...] = jnp.dot(hn, ws_ref[...], preferred_element_type=F32)


def _in_proj(x, gain, w_in, l, *, tm, tn):
    rows, d = x.shape
    ch = w_in.shape[2] // 3
    nj = ch // tn
    return pl.pallas_call(
        _in_proj_kernel,
        grid=(rows // tm, nj),
        in_specs=[
            pl.BlockSpec((tm, d), lambda i, j: (i, 0)),
            pl.BlockSpec((None, 1, d), lambda i, j: (l, 0, 0)),
            pl.BlockSpec((None, d, tn), lambda i, j: (l, 0, j)),
            pl.BlockSpec((None, d, tn), lambda i, j: (l, 0, nj + j)),
            pl.BlockSpec((None, d, tn), lambda i, j: (l, 0, 2 * nj + j)),
        ],
        out_specs=[
            pl.BlockSpec((tm, tn), lambda i, j: (i, j)),
            pl.BlockSpec((tm, tn), lambda i, j: (i, j)),
        ],
        out_shape=[jax.ShapeDtypeStruct((rows, ch), F32),
                   jax.ShapeDtypeStruct((rows, ch), F32)],
        scratch_shapes=[pltpu.VMEM((tm, d), BF16)],
        compiler_params=_cparams(("parallel", "arbitrary")),
        name="in_proj",
    )(x, gain, w_in, w_in, w_in)


def _conv_tail(c, b_ref, lng_ref, lnb_ref, bg_ref):
    c = c + b_ref[...]
    xc = c - jnp.mean(c, axis=-1, keepdims=True)
    var = jnp.mean(xc * xc, axis=-1, keepdims=True)
    y = xc * lax.rsqrt(var + EPS) * lng_ref[...] + lnb_ref[...]
    y = y * jax.nn.sigmoid(y)
    return _rms_rows(y, bg_ref[...])


def _conv_prompt_kernel(cur_ref, prev_ref, wb_ref, b_ref, lng_ref, lnb_ref, bg_ref, dst_any_ref,
                        o_ref, ext_ref, c_ref):
    del dst_any_ref
    t, ch = cur_ref.shape
    rows_e = CONV_HALO + t
    first = pl.program_id(1) == 0
    ext_ref[0, 0:CONV_HALO, :] = jnp.where(first, 0.0, prev_ref[...])
    ext_ref[0, CONV_HALO:rows_e, :] = cur_ref[...]
    for r in range(1, 8):
        ext_ref[r] = pltpu.roll(ext_ref[0], rows_e - r, axis=0)
    off = CONV_HALO - (CONV_W - 1)
    n_rb = t // 8
    for lc in range(ch // LANES):
        lanes = slice(lc * LANES, (lc + 1) * LANES)
        w = [wb_ref[k, :, lanes] for k in range(CONV_W)]
        acc = [None] * n_rb
        for r in range(8):
            taps = [(k, (off + k) // 8) for k in range(CONV_W) if (off + k) % 8 == r]
            for m in range(rows_e // 8):
                uses = [(k, m - a) for k, a in taps if 0 <= m - a < n_rb]
                if not uses:
                    continue
                v = ext_ref[r, m * 8:(m + 1) * 8, lanes]
                for k, rb in uses:
                    term = w[k] * v
                    acc[rb] = term if acc[rb] is None else acc[rb] + term
        for rb in range(n_rb):
            c_ref[rb * 8:(rb + 1) * 8, lanes] = acc[rb]
    o_ref[...] = _conv_tail(c_ref[...], b_ref, lng_ref, lnb_ref, bg_ref).astype(o_ref.dtype)


def _conv_prompt(g_all, conv_wb, conv_b, ln_g, ln_b, bg, dst, l, *, nb, seq):
    rows, ch = g_all.shape
    t = CONV_T
    per_seq = seq // t
    vec = lambda: pl.BlockSpec((None, 1, ch), lambda b, i: (l, 0, 0))
    return pl.pallas_call(
        _conv_prompt_kernel,
        grid=(nb, per_seq),
        in_specs=[
            pl.BlockSpec((t, ch), lambda b, i: (b * per_seq + i, 0)),
            pl.BlockSpec((CONV_HALO, ch),
                         lambda b, i: (jnp.maximum((b * seq + i * t) // CONV_HALO - 1, 0), 0)),
            pl.BlockSpec((None, CONV_W, 8, ch), lambda b, i: (l, 0, 0, 0)),
            vec(), vec(), vec(), vec(),
            pl.BlockSpec(memory_space=pl.ANY),
        ],
        out_specs=pl.BlockSpec((t, ch), lambda b, i: (b * per_seq + i, 0)),
        out_shape=jax.ShapeDtypeStruct(dst.shape, dst.dtype),
        input_output_aliases={7: 0},
        scratch_shapes=[pltpu.VMEM((8, CONV_HALO + t, ch), F32), pltpu.VMEM((t, ch), F32)],
        compiler_params=_cparams(("parallel", "arbitrary")),
        name="conv_prompt",
    )(g_all, g_all, conv_wb, conv_b, ln_g, ln_b, bg, dst)


def _conv_sample_kernel(cache_ref, g_ref, w_ref, b_ref, lng_ref, lnb_ref, bg_ref, *refs):
    cache_out_ref, cn_ref = refs[-2:]
    hist = CONV_W - 1
    g = g_ref[...]
    c = w_ref[hist:CONV_W, :] * g
    for k in range(hist):
        c = c + w_ref[k:k + 1, :] * cache_ref[k]
    cn_ref[...] = _conv_tail(c, b_ref, lng_ref, lnb_ref, bg_ref).astype(cn_ref.dtype)
    cache_out_ref[0:hist - 1] = cache_ref[1:hist]
    cache_out_ref[hist - 1] = g


def _conv_sample(cache_t, g_all, conv_w, conv_b, ln_g, ln_b, bg, cn_all, cache_out, l, *, row0, bs):
    _, hist, ns, ch = cache_t.shape
    vec = lambda: pl.BlockSpec((None, 1, ch), lambda j: (l, 0, 0))
    return pl.pallas_call(
        _conv_sample_kernel,
        grid=(ns // bs,),
        in_specs=[
            pl.BlockSpec((None, hist, bs, ch), lambda j: (l, 0, j, 0)),
            pl.BlockSpec((bs, ch), lambda j: (row0 // bs + j, 0)),
            pl.BlockSpec((None, CONV_W, ch), lambda j: (l, 0, 0)),
            vec(), vec(), vec(), vec(),
            pl.BlockSpec(memory_space=pl.ANY),
            pl.BlockSpec(memory_space=pl.ANY),
        ],
        out_specs=[
            pl.BlockSpec((None, hist, bs, ch), lambda j: (l, 0, j, 0)),
            pl.BlockSpec((bs, ch), lambda j: (row0 // bs + j, 0)),
        ],
        out_shape=[jax.ShapeDtypeStruct(cache_t.shape, F32),
                   jax.ShapeDtypeStruct(cn_all.shape, cn_all.dtype)],
        input_output_aliases={7: 1, 8: 0},
        compiler_params=_cparams(("parallel",)),
        name="conv_sample",
    )(cache_t, g_all, conv_w, conv_b, ln_g, ln_b, bg, cn_all, cache_out)


def _ssm_kernel(u_ref, are_ref, aim_ref, ldt_ref, b1_ref, b2_ref, c1_ref, c2_ref, d_ref,
                hre_in_ref, him_in_ref,
                gy_ref, hre_p_ref, him_p_ref, hre_s_ref, him_s_ref,
                ws_ref, wit_ref, dst_ref, x_ref, s_ref, hp_ref, hpb_ref, *, nb, nc, ns):
    t_len = SSM_T
    half = SSM_SW // 2
    rows_c = nb * nc
    row_s = rows_c * t_len

    a_re = are_ref[...]
    a_im = aim_ref[...]
    dt = jnp.exp(ldt_ref[...])
    mag = jnp.exp(a_re * dt)
    ang = a_im * dt
    abar_r = mag * jnp.cos(ang)
    abar_i = mag * jnp.sin(ang)
    nr = abar_r - 1.0
    ni = abar_i
    den = a_re * a_re + a_im * a_im
    cf_r = (nr * a_re + ni * a_im) / den
    cf_i = (ni * a_re - nr * a_im) / den

    pw = [(jnp.ones_like(abar_r), jnp.zeros_like(abar_r))]
    for _ in range(t_len):
        pr, pi = pw[-1]
        pw.append((pr * abar_r - pi * abar_i, pr * abar_i + pi * abar_r))

    b1 = b1_ref[...]
    b2 = b2_ref[...]
    c1 = c1_ref[...]
    c2 = c2_ref[...]
    c1_hi = c1.astype(BF16)
    c1_lo = (c1 - c1_hi.astype(F32)).astype(BF16)
    dst_ref[(t_len - 1) * LANES:, 0:LANES] = jnp.zeros((LANES, LANES), BF16)
    for t in range(t_len):
        sl = slice(t * LANES, (t + 1) * LANES)
        pr, pi = pw[t_len - 1 - t]
        acr = pr * cf_r - pi * cf_i
        aci = pr * cf_i + pi * cf_r
        ws_t = acr * b1 + aci * b2
        ws_hi = ws_t.astype(BF16)
        ws_lo = (ws_t - ws_hi.astype(F32)).astype(BF16)
        ws_ref[sl, :] = ws_hi
        lag = (_nt_dot(ws_hi, c1_hi) + _nt_dot(ws_hi, c1_lo) + _nt_dot(ws_lo, c1_hi)).astype(BF16)
        dst_ref[sl, LANES:] = lag
        if t > 0:
            dst_ref[(t - 1) * LANES:t * LANES, 0:LANES] = lag
        pr, pi = pw[t + 1]
        wit_ref[sl, :] = (pr * c1 + pi * c2).astype(BF16)
        x_ref[:, sl] = u_ref[pl.ds(t, rows_c, stride=t_len), :].astype(BF16)

    s_ref[...] = jnp.dot(x_ref[...], ws_ref[...], preferred_element_type=F32)
    at_r = pw[t_len][0][:, :half]
    at_i = pw[t_len][1][:, :half]

    def scan_body(c, carry):
        new = []
        for b in range(nb):
            hr, hi = carry[b]
            row = pl.ds(b * nc + c, 1)
            hp_ref[row, 0:half] = hr
            hp_ref[row, half:] = hi
            sr = s_ref[row, 0:half]
            si = s_ref[row, half:]
            new.append((at_r * hr - at_i * hi + sr, at_r * hi + at_i * hr + si))
        return tuple(new)

    zero = jnp.zeros((1, half), F32)
    final = lax.fori_loop(0, nc, scan_body, tuple((zero, zero) for _ in range(nb)))
    for b in range(nb):
        hre_p_ref[b:b + 1, :] = final[b][0]
        him_p_ref[b:b + 1, :] = final[b][1]
    hpb_ref[...] = hp_ref[...].astype(BF16)

    d_skip = d_ref[...]
    for t in range(0, t_len, 2):
        y = jnp.dot(x_ref[:, 0:(t + 2) * LANES], dst_ref[(t_len - 2 - t) * LANES:, :],
                    preferred_element_type=F32)
        y = y + _nt_dot(hpb_ref[...], wit_ref[t * LANES:(t + 2) * LANES, :])
        for q in range(2):
            u_t = u_ref[pl.ds(t + q, rows_c, stride=t_len), :]
            gy_ref[pl.ds(t + q, rows_c, stride=t_len), :] = jax.nn.gelu(
                y[:, q * LANES:(q + 1) * LANES] + d_skip * u_t)

    u_s = u_ref[row_s:row_s + ns, :]
    bu = jnp.dot(u_s.astype(BF16), ws_ref[(t_len - 1) * LANES:, :], preferred_element_type=F32)
    a1_r = abar_r[:, :half]
    a1_i = abar_i[:, :half]
    hre = hre_in_ref[...]
    him = him_in_ref[...]
    nre = a1_r * hre - a1_i * him + bu[:, :half]
    nim = a1_r * him + a1_i * hre + bu[:, half:]
    hre_s_ref[...] = nre
    him_s_ref[...] = nim
    h_cat = jnp.concatenate([nre, nim], axis=1).astype(BF16)
    y_s = _nt_dot(h_cat, c1.astype(BF16))
    gy_ref[row_s:row_s + ns, :] = jax.nn.gelu(y_s + d_skip * u_s)


def _ssm_weights(a_re, a_im, log_dt, b_re, b_im, c_re, c_im, ssm_d):
    nl, g, n = a_re.shape
    p = b_re.shape[3]
    nbd = g // SSM_GB

    def rowvec(a):
        a = a.reshape(nl, nbd, 1, SSM_GB * n)
        return jnp.concatenate([a, a], axis=-1)

    eye = jnp.eye(SSM_GB, dtype=bool)[None, None, :, None, :, None]

    def blockdiag(m):
        e = jnp.where(eye, m[:, :, :, :, None, :], 0.0)
        return e.reshape(nl, nbd, SSM_GB * p, SSM_GB * n)

    bre = blockdiag(b_re.reshape(nl, nbd, SSM_GB, n, p).transpose(0, 1, 2, 4, 3))
    bim = blockdiag(b_im.reshape(nl, nbd, SSM_GB, n, p).transpose(0, 1, 2, 4, 3))
    cre = blockdiag(c_re.reshape(nl, nbd, SSM_GB, p, n))
    cim = blockdiag(c_im.reshape(nl, nbd, SSM_GB, p, n))
    return dict(
        are=rowvec(a_re), aim=rowvec(a_im),
        ldt=rowvec(jnp.broadcast_to(log_dt[:, :, None], (nl, g, n))),
        b1=jnp.concatenate([bre, bim], axis=-1), b2=jnp.concatenate([-bim, bre], axis=-1),
        c1=jnp.concatenate([cre, -cim], axis=-1), c2=jnp.concatenate([-cim, -cre], axis=-1),
        d=ssm_d.reshape(nl, nbd, 1, SSM_GB * p),
    )


def _ssm(s_all, sw, state_re, state_im, l, *, nb, seq, ns):
    rows, ch = s_all.shape
    nbd = ch // LANES
    nc = seq // SSM_T
    rows_c = nb * nc
    half = SSM_SW // 2
    gp = SSM_GB * SSM_P
    row = lambda w: pl.BlockSpec((None, None, 1, w), lambda j: (l, j, 0, 0))
    mat = lambda: pl.BlockSpec((None, None, gp, SSM_SW), lambda j: (l, j, 0, 0))
    kern = functools.partial(_ssm_kernel, nb=nb, nc=nc, ns=ns)
    return pl.pallas_call(
        kern,
        grid=(nbd,),
        in_specs=[
            pl.BlockSpec((rows, LANES), lambda j: (0, j)),
            row(SSM_SW), row(SSM_SW), row(SSM_SW),
            mat(), mat(), mat(), mat(),
            row(gp),
            pl.BlockSpec((None, ns, half), lambda j: (l, 0, j)),
            pl.BlockSpec((None, ns, half), lambda j: (l, 0, j)),
        ],
        out_specs=[
            pl.BlockSpec((rows, LANES), lambda j: (0, j)),
            pl.BlockSpec((nb, half), lambda j: (0, j)),
            pl.BlockSpec((nb, half), lambda j: (0, j)),
            pl.BlockSpec((ns, half), lambda j: (0, j)),
            pl.BlockSpec((ns, half), lambda j: (0, j)),
        ],
        out_shape=[
            jax.ShapeDtypeStruct((rows, ch), F32),
            jax.ShapeDtypeStruct((nb, nbd * half), F32),
            jax.ShapeDtypeStruct((nb, nbd * half), F32),
            jax.ShapeDtypeStruct((ns, nbd * half), F32),
            jax.ShapeDtypeStruct((ns, nbd * half), F32),
        ],
        scratch_shapes=[
            pltpu.VMEM((SSM_T * LANES, SSM_SW), BF16),
            pltpu.VMEM((SSM_T * LANES, SSM_SW), BF16),
            pltpu.VMEM((SSM_T * LANES, 2 * LANES), BF16),
            pltpu.VMEM((rows_c, SSM_T * LANES), BF16),
            pltpu.VMEM((rows_c, SSM_SW), F32),
            pltpu.VMEM((rows_c, SSM_SW), F32),
            pltpu.VMEM((rows_c, SSM_SW), BF16),
        ],
        compiler_params=_cparams(("parallel",)),
        name="ssm",
    )(s_all, sw["are"], sw["aim"], sw["ldt"], sw["b1"], sw["b2"], sw["c1"], sw["c2"], sw["d"],
      state_re, state_im)


def _glu_norm_kernel(gy_ref, w_ref, bg_ref, o_ref):
    gy = gy_ref[...]
    z = jnp.dot(gy.astype(BF16), w_ref[...], preferred_element_type=F32)
    o_ref[...] = _rms_rows(gy * jax.nn.sigmoid(z), bg_ref[...]).astype(o_ref.dtype)


def _glu_norm(gy, w_glu, bg, l, *, tm):
    rows, ch = gy.shape
    return pl.pallas_call(
        _glu_norm_kernel,
        grid=(rows // tm,),
        in_specs=[
            pl.BlockSpec((tm, ch), lambda i: (i, 0)),
            pl.BlockSpec((None, ch, ch), lambda i: (l, 0, 0)),
            pl.BlockSpec((None, 1, ch), lambda i: (l, 0, 0)),
        ],
        out_specs=pl.BlockSpec((tm, ch), lambda i: (i, 0)),
        out_shape=jax.ShapeDtypeStruct((rows, ch), BF16),
        compiler_params=_cparams(("parallel",)),
        name="glu_norm",
    )(gy, w_glu, bg)


def _mix_out_kernel(c_ref, s_ref, wc_ref, ws_ref, x_ref, o_ref):
    acc = jnp.dot(c_ref[...], wc_ref[...], preferred_element_type=F32)
    acc = acc + jnp.dot(s_ref[...], ws_ref[...], preferred_element_type=F32)
    o_ref[...] = x_ref[...] + acc


def _mix_out(cn, sn, w_out, x, l, *, tm, tn):
    rows, d = x.shape
    ch = cn.shape[1]
    return pl.pallas_call(
        _mix_out_kernel,
        grid=(rows // tm, d // tn),
        in_specs=[
            pl.BlockSpec((tm, ch), lambda i, j: (i, 0)),
            pl.BlockSpec((tm, ch), lambda i, j: (i, 0)),
            pl.BlockSpec((None, ch, tn), lambda i, j: (l, 0, j)),
            pl.BlockSpec((None, ch, tn), lambda i, j: (l, 1, j)),
            pl.BlockSpec((tm, tn), lambda i, j: (i, j)),
        ],
        out_specs=pl.BlockSpec((tm, tn), lambda i, j: (i, j)),
        out_shape=jax.ShapeDtypeStruct((rows, d), F32),
        compiler_params=_cparams(("parallel", "arbitrary")),
        name="mix_out",
    )(cn, sn, w_out, w_out, x)


def _norm_proj_kernel(x_ref, gain_ref, *refs):
    nw = (len(refs) - 1) // 2
    w_refs, o_refs, hn_ref = refs[:nw], refs[nw:2 * nw], refs[2 * nw]
    _norm_rows_to(x_ref, gain_ref, hn_ref, x_ref.shape[0])
    hn = hn_ref[...]
    for w_ref, o_ref in zip(w_refs, o_refs):
        o_ref[...] = jnp.dot(hn, w_ref[...], preferred_element_type=F32)


def _norm_proj(x, gain, weights, l, *, tm, name, row0=0, nrows=None):
    d = x.shape[1]
    nrows = x.shape[0] - row0 if nrows is None else nrows
    n = weights[0].shape[2]
    blk0 = row0 // tm
    return pl.pallas_call(
        _norm_proj_kernel,
        grid=(nrows // tm,),
        in_specs=[pl.BlockSpec((tm, d), lambda i: (blk0 + i, 0)),
                  pl.BlockSpec((None, 1, d), lambda i: (l, 0, 0))]
                 + [pl.BlockSpec((None, d, n), lambda i: (l, 0, 0)) for _ in weights],
        out_specs=[pl.BlockSpec((tm, n), lambda i: (i, 0)) for _ in weights],
        out_shape=[jax.ShapeDtypeStruct((nrows, n), F32) for _ in weights],
        scratch_shapes=[pltpu.VMEM((tm, d), BF16)],
        compiler_params=_cparams(("parallel",)),
        name=name,
    )(x, gain, *weights)


def _proj_res_kernel(a_ref, w_ref, x_ref, o_ref):
    o_ref[...] = x_ref[...] + jnp.dot(a_ref[...].astype(BF16), w_ref[...],
                                      preferred_element_type=F32)


def _proj_res_rows(a, w, x, l, *, row0, tn, name):
    d = x.shape[1]
    m, k = a.shape
    blk0 = row0 // m
    return pl.pallas_call(
        _proj_res_kernel,
        grid=(d // tn,),
        in_specs=[
            pl.BlockSpec((m, k), lambda j: (0, 0)),
            pl.BlockSpec((None, k, tn), lambda j: (l, 0, j)),
            pl.BlockSpec((m, tn), lambda j: (blk0, j)),
        ],
        out_specs=pl.BlockSpec((m, tn), lambda j: (blk0, j)),
        out_shape=jax.ShapeDtypeStruct(x.shape, x.dtype),
        input_output_aliases={2: 0},
        compiler_params=_cparams(("parallel",)),
        name=name,
    )(a, w, x)


def _softmax_rows(s):
    p = jnp.exp(s - jnp.max(s, axis=-1, keepdims=True))
    return p / jnp.sum(p, axis=-1, keepdims=True)


def _attn_prompt_kernel(x_ref, gain_ref, wq_ref, k_ref, v_ref, wo_ref, o_ref, hn_ref, oc_ref,
                        *, heads, dh):
    _norm_rows_to(x_ref, gain_ref, hn_ref, x_ref.shape[0])
    q = jnp.dot(hn_ref[...], wq_ref[...], preferred_element_type=F32)
    scale = dh ** -0.5
    for h in range(heads):
        sl = slice(h * dh, (h + 1) * dh)
        s = _nt_dot(q[:, sl].astype(BF16), k_ref[:, sl].astype(BF16)) * scale
        w = _softmax_rows(s)
        oc_ref[:, sl] = jnp.dot(w.astype(BF16), v_ref[:, sl].astype(BF16),
                                preferred_element_type=F32).astype(BF16)
    o_ref[...] = x_ref[...] + jnp.dot(oc_ref[...], wo_ref[...], preferred_element_type=F32)


def _attn_prompt(x, gain, w_xq, k, v, w_xo, l, *, nb, seq, tq, heads):
    rows, d = x.shape
    width = k.shape[1]
    n_mem = k.shape[0] // nb
    per_seq = seq // tq
    kern = functools.partial(_attn_prompt_kernel, heads=heads, dh=width // heads)
    return pl.pallas_call(
        kern,
        grid=(nb, per_seq),
        in_specs=[
            pl.BlockSpec((tq, d), lambda b, i: (b * per_seq + i, 0)),
            pl.BlockSpec((None, 1, d), lambda b, i: (l, 0, 0)),
            pl.BlockSpec((None, d, width), lambda b, i: (l, 0, 0)),
            pl.BlockSpec((n_mem, width), lambda b, i: (b, 0)),
            pl.BlockSpec((n_mem, width), lambda b, i: (b, 0)),
            pl.BlockSpec((None, width, d), lambda b, i: (l, 0, 0)),
        ],
        out_specs=pl.BlockSpec((tq, d), lambda b, i: (b * per_seq + i, 0)),
        out_shape=jax.ShapeDtypeStruct((rows, d), F32),
        input_output_aliases={0: 0},
        scratch_shapes=[pltpu.VMEM((tq, d), BF16), pltpu.VMEM((tq, width), BF16)],
        compiler_params=_cparams(("parallel", "arbitrary")),
        name="attn_prompt",
    )(x, gain, w_xq, k, v, w_xo)


def _attn_sample_kernel(q_ref, k_ref, v_ref, o_ref, *, heads, dh):
    bs = q_ref.shape[0]
    nr = k_ref.shape[1]
    hp = 8
    q = q_ref[...]
    head_row = lax.broadcasted_iota(jnp.int32, (bs, hp, dh), 1)
    q4 = jnp.zeros((bs, hp, dh), F32)
    for h in range(heads):
        q4 = jnp.where(head_row == h, q[:, None, h * dh:(h + 1) * dh], q4)
    s = jnp.einsum("bhd,brd->bhr", q4.astype(BF16), k_ref[...].astype(BF16),
                   preferred_element_type=F32) * (dh ** -0.5)
    own = (lax.broadcasted_iota(jnp.int32, (bs, hp, nr), 2) % heads
           == lax.broadcasted_iota(jnp.int32, (bs, hp, nr), 1))
    w = _softmax_rows(jnp.where(own, s, -1e30))
    o = jnp.einsum("bhr,brd->bhd", w.astype(BF16), v_ref[...].astype(BF16),
                   preferred_element_type=F32)
    for h in range(heads):
        o_ref[:, h * dh:(h + 1) * dh] = o[:, h, :]


def _attn_sample(q, k_cache, v_cache, l, *, bs, heads):
    _, ns, nr, dh = k_cache.shape
    width = heads * dh
    kern = functools.partial(_attn_sample_kernel, heads=heads, dh=dh)
    return pl.pallas_call(
        kern,
        grid=(ns // bs,),
        in_specs=[
            pl.BlockSpec((bs, width), lambda j: (j, 0)),
            pl.BlockSpec((None, bs, nr, dh), lambda j: (l, j, 0, 0)),
            pl.BlockSpec((None, bs, nr, dh), lambda j: (l, j, 0, 0)),
        ],
        out_specs=pl.BlockSpec((bs, width), lambda j: (j, 0)),
        out_shape=jax.ShapeDtypeStruct((ns, width), F32),
        compiler_params=_cparams(("parallel",)),
        name="attn_sample",
    )(q, k_cache, v_cache)


def _ffn_kernel(x_hbm_ref, gain_ref, wu_ref, wd_ref, o_ref, hn_ref, sem):
    tm = o_ref.shape[0]

    @pl.when(pl.program_id(1) == 0)
    def _():
        r0 = pl.multiple_of(pl.program_id(0) * tm, tm)
        cp = pltpu.make_async_copy(x_hbm_ref.at[pl.ds(r0, tm), :], o_ref, sem)
        cp.start()
        cp.wait()
        _norm_rows_to(o_ref, gain_ref, hn_ref, tm)

    wu = wu_ref[...].astype(BF16)
    hm = tm // 2
    h = jnp.concatenate(
        [jnp.dot(hn_ref[0:hm, :], wu, preferred_element_type=F32),
         jnp.dot(hn_ref[hm:tm, :], wu, preferred_element_type=F32)], axis=0)
    h = jnp.square(jnp.maximum(h, 0.0)).astype(BF16)
    o_ref[...] += jnp.dot(h, wd_ref[...].astype(BF16), preferred_element_type=F32)


def _ffn(x, gain, w_up, w_down, l, *, tm, tf):
    rows, d = x.shape
    dff = w_up.shape[2]
    return pl.pallas_call(
        _ffn_kernel,
        grid=(rows // tm, dff // tf),
        in_specs=[
            pl.BlockSpec(memory_space=pl.ANY),
            pl.BlockSpec((None, 1, d), lambda i, k: (l, 0, 0)),
            pl.BlockSpec((None, d, tf), lambda i, k: (l, 0, k)),
            pl.BlockSpec((None, tf, d), lambda i, k: (l, k, 0)),
        ],
        out_specs=pl.BlockSpec((tm, d), lambda i, k: (i, 0)),
        out_shape=jax.ShapeDtypeStruct((rows, d), F32),
        scratch_shapes=[pltpu.VMEM((tm, d), BF16), pltpu.SemaphoreType.DMA(())],
        compiler_params=pltpu.CompilerParams(
            dimension_semantics=("arbitrary", "arbitrary"),
            vmem_limit_bytes=V7X_FFN_VMEM_LIMIT_BYTES),
        name="ffn",
    )(x, gain, w_up, w_down)


def _final_norm_kernel(x_ref, gain_ref, o_ref):
    o_ref[...] = _rms_rows(x_ref[...], gain_ref[...])


def _final_norm(x, gain, *, row0, nrows, tm):
    d = x.shape[1]
    return pl.pallas_call(
        _final_norm_kernel,
        grid=(nrows // tm,),
        in_specs=[pl.BlockSpec((tm, d), lambda i: (row0 // tm + i, 0)),
                  pl.BlockSpec((1, d), lambda i: (0, 0))],
        out_specs=pl.BlockSpec((tm, d), lambda i: (i, 0)),
        out_shape=jax.ShapeDtypeStruct((nrows, d), F32),
        compiler_params=_cparams(("parallel",)),
        name="final_norm",
    )(x, gain)


def kernel(x_prompt, x_sample, mem_prompt, cache_conv, state_ssm_re, state_ssm_im, cache_mem_k, cache_mem_v, norm_mix_g, w_in, conv_w, conv_b, conv_ln_g, conv_ln_b, ssm_a_re, ssm_a_im, ssm_log_dt, ssm_b_re, ssm_b_im, ssm_c_re, ssm_c_im, ssm_d, w_glu, branch_g_conv, branch_g_ssm, w_out, norm_x_g, norm_mem_g, w_xq, w_xk, w_xv, w_xo, norm_ffn_g, w_up, w_down, norm_final_g):
    nb, seq, d = x_prompt.shape
    ns = x_sample.shape[0]
    depth = w_in.shape[0]
    n_mem = mem_prompt.shape[1]
    heads, dh = cache_mem_k.shape[3], cache_mem_k.shape[4]
    width = heads * dh
    ch = conv_w.shape[2]
    g_ssm, n_state = ssm_a_re.shape[1], ssm_a_re.shape[2]
    rows_p = nb * seq
    rows = rows_p + ns

    x = jnp.concatenate([x_prompt.reshape(rows_p, d), x_sample.reshape(ns, d)], axis=0)
    mem = mem_prompt.reshape(nb * n_mem, d)
    vec = lambda a: a.reshape(depth, 1, a.shape[-1])
    mix_g, x_g, mem_g, ffn_g = vec(norm_mix_g), vec(norm_x_g), vec(norm_mem_g), vec(norm_ffn_g)
    cb, lng, lnb, bgc, bgs = (vec(conv_b), vec(conv_ln_g), vec(conv_ln_b),
                              vec(branch_g_conv), vec(branch_g_ssm))
    w_in_b, w_glu_b, w_out_b = w_in.astype(BF16), w_glu.astype(BF16), w_out.astype(BF16)
    w_xq_b, w_xk_b, w_xv_b, w_xo_b = (w_xq.astype(BF16), w_xk.astype(BF16),
                                      w_xv.astype(BF16), w_xo.astype(BF16))
    sw = _ssm_weights(ssm_a_re, ssm_a_im, ssm_log_dt, ssm_b_re, ssm_b_im, ssm_c_re, ssm_c_im, ssm_d)
    st_re = state_ssm_re.reshape(depth, ns, g_ssm * n_state)
    st_im = state_ssm_im.reshape(depth, ns, g_ssm * n_state)
    k_cache = cache_mem_k.reshape(depth, ns, n_mem * heads, dh)
    v_cache = cache_mem_v.reshape(depth, ns, n_mem * heads, dh)
    cache_t = jnp.transpose(cache_conv, (0, 2, 1, 3))

    conv_wb = jnp.broadcast_to(conv_w[:, :, None, :], (depth, CONV_W, 8, ch))
    tm = _pick_tile(rows, 640, NORM_CHUNK)
    cache_out = jnp.zeros(cache_t.shape, F32)
    cn = jnp.zeros((rows, ch), BF16)
    mk_p, mv_p, cb_p, sr_p, si_p, sr_s, si_s = [], [], [], [], [], [], []
    for l in range(depth):
        g_all, s_all = _in_proj(x, mix_g, w_in_b, l, tm=tm, tn=_pick_tile(ch, 256, LANES))
        cn = _conv_prompt(g_all, conv_wb, cb, lng, lnb, bgc, cn, l, nb=nb, seq=seq)
        cache_out, cn = _conv_sample(cache_t, g_all, conv_w, cb, lng, lnb, bgc, cn, cache_out, l,
                                     row0=rows_p, bs=16)
        gy, hre_p, him_p, hre_s, him_s = _ssm(s_all, sw, st_re, st_im, l, nb=nb, seq=seq, ns=ns)
        sn = _glu_norm(gy, w_glu_b, bgs, l, tm=_pick_tile(rows, 320, NORM_CHUNK))
        x = _mix_out(cn, sn, w_out_b, x, l, tm=tm, tn=_pick_tile(d, 512, LANES))
        k_p, v_p = _norm_proj(mem, mem_g, [w_xk_b, w_xv_b], l,
                              tm=_pick_tile(nb * n_mem, 512, NORM_CHUNK), name="mem_kv")
        x = _attn_prompt(x, x_g, w_xq_b, k_p, v_p, w_xo_b, l, nb=nb, seq=seq,
                         tq=_pick_tile(seq, 256, NORM_CHUNK), heads=heads)
        (q_s,) = _norm_proj(x, x_g, [w_xq_b], l, tm=ns, name="q_proj_sample",
                            row0=rows_p, nrows=ns)
        o_s = _attn_sample(q_s, k_cache, v_cache, l, bs=8, heads=heads)
        x = _proj_res_rows(o_s, w_xo_b, x, l, row0=rows_p,
                           tn=_pick_tile(d, 2048, LANES), name="attn_out_sample")
        x = _ffn(x, ffn_g, w_up, w_down, l, tm=_pick_tile(rows, 832, NORM_CHUNK),
                 tf=_pick_tile(w_up.shape[2], 256, LANES))

        mk_p.append(k_p.reshape(nb, n_mem, heads, dh))
        mv_p.append(v_p.reshape(nb, n_mem, heads, dh))
        cb_p.append(jnp.stack([g_all[(b + 1) * seq - (CONV_W - 1):(b + 1) * seq]
                               for b in range(nb)]))
        sr_p.append(hre_p.reshape(nb, g_ssm, n_state))
        si_p.append(him_p.reshape(nb, g_ssm, n_state))
        sr_s.append(hre_s.reshape(ns, g_ssm, n_state))
        si_s.append(him_s.reshape(ns, g_ssm, n_state))

    gain = norm_final_g.reshape(1, d)
    y_prompt = _final_norm(x, gain, row0=0, nrows=rows_p, tm=512).reshape(nb, seq, d)
    y_sample = _final_norm(x, gain, row0=rows_p, nrows=ns, tm=ns).reshape(ns, 1, d)
    return (y_prompt, y_sample, jnp.stack(mk_p), jnp.stack(mv_p), jnp.stack(cb_p),
            jnp.stack(sr_p), jnp.stack(si_p), jnp.transpose(cache_out, (0, 2, 1, 3)),
            jnp.stack(sr_s), jnp.stack(si_s))
```

```python
import functools

import jax
import jax.numpy as jnp
from jax import lax
from jax.experimental import pallas as pl
from jax.experimental.pallas import tpu as pltpu

F32 = jnp.float32
BF16 = jnp.bfloat16
EPS = 1e-6

V7X_VMEM_LIMIT_BYTES = 56 * 1024 * 1024
V7X_FFN_VMEM_LIMIT_BYTES = 60 * 1024 * 1024
LANES = 128
NORM_CHUNK = 32

CONV_W = 31
CONV_HALO = 32
CONV_T = 128

SSM_T = 16
SSM_GB = 8
SSM_P = 16
SSM_N = 64
SSM_SW = 2 * SSM_GB * SSM_N


def _cparams(sem):
    return pltpu.CompilerParams(dimension_semantics=sem,
                                vmem_limit_bytes=V7X_VMEM_LIMIT_BYTES)


def _rms_rows(x, g):
    ms = jnp.mean(x * x, axis=-1, keepdims=True)
    return x * lax.rsqrt(ms + EPS) * g


def _norm_rows_to(x_ref, gain_ref, hn_ref, rows):
    gain = gain_ref[...]
    chunk = next(c for c in (4 * NORM_CHUNK, 2 * NORM_CHUNK, NORM_CHUNK) if rows % c == 0)

    def body(c, carry):
        r = pl.multiple_of(c * chunk, chunk)
        hn_ref[pl.ds(r, chunk), :] = _rms_rows(
            x_ref[pl.ds(r, chunk), :], gain).astype(hn_ref.dtype)
        return carry

    lax.fori_loop(0, rows // chunk, body, 0)


def _pick_tile(n, target, mult):
    best = None
    for t in range(mult, min(n, target) + 1, mult):
        if n % t == 0:
            best = t
    if best is None:
        raise ValueError(f"no tile for extent {n} (multiple of {mult}, <= {target})")
    return best


def _nt_dot(a, b):
    return lax.dot_general(a, b, (((1,), (1,)), ((), ())),
                           preferred_element_type=F32)


def _in_proj_kernel(x_ref, gain_ref, wv_ref, wg_ref, ws_ref, g_ref, s_ref, hn_ref):
    @pl.when(pl.program_id(1) == 0)
    def _():
        _norm_rows_to(x_ref, gain_ref, hn_ref, x_ref.shape[0])

    hn = hn_ref[...]
    val = jnp.dot(hn, wv_ref[...], preferred_element_type=F32)
    gate = jnp.dot(hn, wg_ref[...], preferred_element_type=F32)
    g_ref[...] = val * jax.nn.sigmoid(gate)
    s_ref[...] = jnp.dot(hn, ws_ref[...], preferred_element_type=F32)


def _in_proj(x, gain, w_in, l, *, tm, tn):
    rows, d = x.shape
    ch = w_in.shape[2] // 3
    nj = ch // tn
    return pl.pallas_call(
        _in_proj_kernel,
        grid=(rows // tm, nj),
        in_specs=[
            pl.BlockSpec((tm, d), lambda i, j: (i, 0)),
            pl.BlockSpec((None, 1, d), lambda i, j: (l, 0, 0)),
            pl.BlockSpec((None, d, tn), lambda i, j: (l, 0, j)),
            pl.BlockSpec((None, d, tn), lambda i, j: (l, 0, nj + j)),
            pl.BlockSpec((None, d, tn), lambda i, j: (l, 0, 2 * nj + j)),
        ],
        out_specs=[
            pl.BlockSpec((tm, tn), lambda i, j: (i, j)),
            pl.BlockSpec((tm, tn), lambda i, j: (i, j)),
        ],
        out_shape=[jax.ShapeDtypeStruct((rows, ch), F32),
                   jax.ShapeDtypeStruct((rows, ch), F32)],
        scratch_shapes=[pltpu.VMEM((tm, d), BF16)],
        compiler_params=_cparams(("parallel", "arbitrary")),
        name="in_proj",
    )(x, gain, w_in, w_in, w_in)


def _conv_tail(c, b_ref, lng_ref, lnb_ref, bg_ref):
    c = c + b_ref[...]
    xc = c - jnp.mean(c, axis=-1, keepdims=True)
    var = jnp.mean(xc * xc, axis=-1, keepdims=True)
    y = xc * lax.rsqrt(var + EPS) * lng_ref[...] + lnb_ref[...]
    y = y * jax.nn.sigmoid(y)
    return _rms_rows(y, bg_ref[...])


def _conv_prompt_kernel(cur_ref, prev_ref, wb_ref, b_ref, lng_ref, lnb_ref, bg_ref, dst_any_ref,
                        o_ref, ext_ref, c_ref):
    del dst_any_ref
    t, ch = cur_ref.shape
    rows_e = CONV_HALO + t
    first = pl.program_id(1) == 0
    ext_ref[0, 0:CONV_HALO, :] = jnp.where(first, 0.0, prev_ref[...])
    ext_ref[0, CONV_HALO:rows_e, :] = cur_ref[...]
    for r in range(1, 8):
        ext_ref[r] = pltpu.roll(ext_ref[0], rows_e - r, axis=0)
    off = CONV_HALO - (CONV_W - 1)
    n_rb = t // 8
    for lc in range(ch // LANES):
        lanes = slice(lc * LANES, (lc + 1) * LANES)
        w = [wb_ref[k, :, lanes] for k in range(CONV_W)]
        acc = [None] * n_rb
        for r in range(8):
            taps = [(k, (off + k) // 8) for k in range(CONV_W) if (off + k) % 8 == r]
            for m in range(rows_e // 8):
                uses = [(k, m - a) for k, a in taps if 0 <= m - a < n_rb]
                if not uses:
                    continue
                v = ext_ref[r, m * 8:(m + 1) * 8, lanes]
                for k, rb in uses:
                    term = w[k] * v
                    acc[rb] = term if acc[rb] is None else acc[rb] + term
        for rb in range(n_rb):
            c_ref[rb * 8:(rb + 1) * 8, lanes] = acc[rb]
    o_ref[...] = _conv_tail(c_ref[...], b_ref, lng_ref, lnb_ref, bg_ref).astype(o_ref.dtype)


def _conv_prompt(g_all, conv_wb, conv_b, ln_g, ln_b, bg, dst, l, *, nb, seq):
    rows, ch = g_all.shape
    t = CONV_T
    per_seq = seq // t
    vec = lambda: pl.BlockSpec((None, 1, ch), lambda b, i: (l, 0, 0))
    return pl.pallas_call(
        _conv_prompt_kernel,
        grid=(nb, per_seq),
        in_specs=[
            pl.BlockSpec((t, ch), lambda b, i: (b * per_seq + i, 0)),
            pl.BlockSpec((CONV_HALO, ch),
                         lambda b, i: (jnp.maximum((b * seq + i * t) // CONV_HALO - 1, 0), 0)),
            pl.BlockSpec((None, CONV_W, 8, ch), lambda b, i: (l, 0, 0, 0)),
            vec(), vec(), vec(), vec(),
            pl.BlockSpec(memory_space=pl.ANY),
        ],
        out_specs=pl.BlockSpec((t, ch), lambda b, i: (b * per_seq + i, 0)),
        out_shape=jax.ShapeDtypeStruct(dst.shape, dst.dtype),
        input_output_aliases={7: 0},
        scratch_shapes=[pltpu.VMEM((8, CONV_HALO + t, ch), F32), pltpu.VMEM((t, ch), F32)],
        compiler_params=_cparams(("parallel", "arbitrary")),
        name="conv_prompt",
    )(g_all, g_all, conv_wb, conv_b, ln_g, ln_b, bg, dst)


def _conv_sample_kernel(cache_ref, g_ref, w_ref, b_ref, lng_ref, lnb_ref, bg_ref, *refs):
    cache_out_ref, cn_ref = refs[-2:]
    hist = CONV_W - 1
    g = g_ref[...]
    c = w_ref[hist:CONV_W, :] * g
    for k in range(hist):
        c = c + w_ref[k:k + 1, :] * cache_ref[k]
    cn_ref[...] = _conv_tail(c, b_ref, lng_ref, lnb_ref, bg_ref).astype(cn_ref.dtype)
    cache_out_ref[0:hist - 1] = cache_ref[1:hist]
    cache_out_ref[hist - 1] = g


def _conv_sample(cache_t, g_all, conv_w, conv_b, ln_g, ln_b, bg, cn_all, cache_out, l, *, row0, bs):
    _, hist, ns, ch = cache_t.shape
    vec = lambda: pl.BlockSpec((None, 1, ch), lambda j: (l, 0, 0))
    return pl.pallas_call(
        _conv_sample_kernel,
        grid=(ns // bs,),
        in_specs=[
            pl.BlockSpec((None, hist, bs, ch), lambda j: (l, 0, j, 0)),
            pl.BlockSpec((bs, ch), lambda j: (row0 // bs + j, 0)),
            pl.BlockSpec((None, CONV_W, ch), lambda j: (l, 0, 0)),
            vec(), vec(), vec(), vec(),
            pl.BlockSpec(memory_space=pl.ANY),
            pl.BlockSpec(memory_space=pl.ANY),
        ],
        out_specs=[
            pl.BlockSpec((None, hist, bs, ch), lambda j: (l, 0, j, 0)),
            pl.BlockSpec((bs, ch), lambda j: (row0 // bs + j, 0)),
        ],
        out_shape=[jax.ShapeDtypeStruct(cache_t.shape, F32),
                   jax.ShapeDtypeStruct(cn_all.shape, cn_all.dtype)],
        input_output_aliases={7: 1, 8: 0},
        compiler_params=_cparams(("parallel",)),
        name="conv_sample",
    )(cache_t, g_all, conv_w, conv_b, ln_g, ln_b, bg, cn_all, cache_out)


def _ssm_kernel(u_ref, are_ref, aim_ref, ldt_ref, bre_ref, bim_ref, cre_ref, cim_ref, d_ref,
                hre_in_ref, him_in_ref,
                gy_ref, hre_p_ref, him_p_ref, hre_s_ref, him_s_ref,
                ws_ref, wit_ref, dst_ref, x_ref, s_ref, hp_ref, hpb_ref, *, nb, nc, ns):
    t_len = SSM_T
    half = SSM_SW // 2
    rows_c = nb * nc
    row_s = rows_c * t_len

    a_re = are_ref[...]
    a_im = aim_ref[...]
    dt = jnp.exp(ldt_ref[...])
    mag = jnp.exp(a_re * dt)
    ang = a_im * dt
    abar_r = mag * jnp.cos(ang)
    abar_i = mag * jnp.sin(ang)
    nr = abar_r - 1.0
    ni = abar_i
    den = a_re * a_re + a_im * a_im
    cf_r = (nr * a_re + ni * a_im) / den
    cf_i = (ni * a_re - nr * a_im) / den

    pw = [(jnp.ones_like(abar_r), jnp.zeros_like(abar_r))]
    for _ in range(t_len):
        pr, pi = pw[-1]
        pw.append((pr * abar_r - pi * abar_i, pr * abar_i + pi * abar_r))

    gp = bre_ref.shape[0]
    own = (lax.broadcasted_iota(jnp.int32, (gp, half), 0) // SSM_P
           == lax.broadcasted_iota(jnp.int32, (gp, half), 1) // SSM_N)

    def expand(m_ref):
        wide = jnp.concatenate([m_ref[...]] * (half // LANES), axis=1)
        return jnp.where(own, wide, 0.0)

    bre, bim, cre, cim = expand(bre_ref), expand(bim_ref), expand(cre_ref), expand(cim_ref)
    b1 = jnp.concatenate([bre, bim], axis=1)
    b2 = jnp.concatenate([-bim, bre], axis=1)
    c1 = jnp.concatenate([cre, -cim], axis=1)
    c2 = jnp.concatenate([-cim, -cre], axis=1)
    c1_hi = c1.astype(BF16)
    c1_lo = (c1 - c1_hi.astype(F32)).astype(BF16)
    dst_ref[(t_len - 1) * LANES:, 0:LANES] = jnp.zeros((LANES, LANES), BF16)
    for t in range(t_len):
        sl = slice(t * LANES, (t + 1) * LANES)
        pr, pi = pw[t_len - 1 - t]
        acr = pr * cf_r - pi * cf_i
        aci = pr * cf_i + pi * cf_r
        ws_t = acr * b1 + aci * b2
        ws_hi = ws_t.astype(BF16)
        ws_lo = (ws_t - ws_hi.astype(F32)).astype(BF16)
        ws_ref[sl, :] = ws_hi
        lag = (_nt_dot(ws_hi, c1_hi) + _nt_dot(ws_hi, c1_lo) + _nt_dot(ws_lo, c1_hi)).astype(BF16)
        dst_ref[sl, LANES:] = lag
        if t > 0:
            dst_ref[(t - 1) * LANES:t * LANES, 0:LANES] = lag
        pr, pi = pw[t + 1]
        wit_ref[sl, :] = (pr * c1 + pi * c2).astype(BF16)
        x_ref[:, sl] = u_ref[pl.ds(t, rows_c, stride=t_len), :].astype(BF16)

    s_ref[...] = jnp.dot(x_ref[...], ws_ref[...], preferred_element_type=F32)
    at_r = pw[t_len][0][:, :half]
    at_i = pw[t_len][1][:, :half]

    def scan_body(c, carry):
        new = []
        for b in range(nb):
            hr, hi = carry[b]
            row = pl.ds(b * nc + c, 1)
            hp_ref[row, 0:half] = hr
            hp_ref[row, half:] = hi
            sr = s_ref[row, 0:half]
            si = s_ref[row, half:]
            new.append((at_r * hr - at_i * hi + sr, at_r * hi + at_i * hr + si))
        return tuple(new)

    zero = jnp.zeros((1, half), F32)
    final = lax.fori_loop(0, nc, scan_body, tuple((zero, zero) for _ in range(nb)))
    for b in range(nb):
        hre_p_ref[b:b + 1, :] = final[b][0]
        him_p_ref[b:b + 1, :] = final[b][1]
    hpb_ref[...] = hp_ref[...].astype(BF16)

    d_skip = d_ref[...]
    for t in range(0, t_len, 2):
        y = jnp.dot(x_ref[:, 0:(t + 2) * LANES], dst_ref[(t_len - 2 - t) * LANES:, :],
                    preferred_element_type=F32)
        y = y + _nt_dot(hpb_ref[...], wit_ref[t * LANES:(t + 2) * LANES, :])
        for q in range(2):
            u_t = u_ref[pl.ds(t + q, rows_c, stride=t_len), :]
            gy_ref[pl.ds(t + q, rows_c, stride=t_len), :] = jax.nn.gelu(
                y[:, q * LANES:(q + 1) * LANES] + d_skip * u_t)

    u_s = u_ref[row_s:row_s + ns, :]
    bu = jnp.dot(u_s.astype(BF16), ws_ref[(t_len - 1) * LANES:, :], preferred_element_type=F32)
    a1_r = abar_r[:, :half]
    a1_i = abar_i[:, :half]
    hre = hre_in_ref[...]
    him = him_in_ref[...]
    nre = a1_r * hre - a1_i * him + bu[:, :half]
    nim = a1_r * him + a1_i * hre + bu[:, half:]
    hre_s_ref[...] = nre
    him_s_ref[...] = nim
    h_cat = jnp.concatenate([nre, nim], axis=1).astype(BF16)
    y_s = _nt_dot(h_cat, c1.astype(BF16))
    gy_ref[row_s:row_s + ns, :] = jax.nn.gelu(y_s + d_skip * u_s)


def _ssm_weights(a_re, a_im, log_dt, b_re, b_im, c_re, c_im, ssm_d):
    nl, g, n = a_re.shape
    p = b_re.shape[3]
    nbd = g // SSM_GB
    assert (n, p, 2 * n) == (SSM_N, SSM_P, LANES)

    def rowvec(a):
        a = a.reshape(nl, nbd, 1, SSM_GB * n)
        return jnp.concatenate([a, a], axis=-1)

    def rows_gp(m):
        m = m.reshape(nl, nbd, SSM_GB * p, n)
        return jnp.concatenate([m, m], axis=-1)

    return dict(
        are=rowvec(a_re), aim=rowvec(a_im),
        ldt=rowvec(jnp.broadcast_to(log_dt[:, :, None], (nl, g, n))),
        bre=rows_gp(b_re.reshape(nl, nbd, SSM_GB, n, p).transpose(0, 1, 2, 4, 3)),
        bim=rows_gp(b_im.reshape(nl, nbd, SSM_GB, n, p).transpose(0, 1, 2, 4, 3)),
        cre=rows_gp(c_re.reshape(nl, nbd, SSM_GB, p, n)),
        cim=rows_gp(c_im.reshape(nl, nbd, SSM_GB, p, n)),
        d=ssm_d.reshape(nl, nbd, 1, SSM_GB * p),
    )


def _ssm(s_all, sw, state_re, state_im, l, *, nb, seq, ns):
    rows, ch = s_all.shape
    nbd = ch // LANES
    nc = seq // SSM_T
    rows_c = nb * nc
    half = SSM_SW // 2
    gp = SSM_GB * SSM_P
    row = lambda w: pl.BlockSpec((None, None, 1, w), lambda j: (l, j, 0, 0))
    mat = lambda: pl.BlockSpec((None, None, gp, LANES), lambda j: (l, j, 0, 0))
    kern = functools.partial(_ssm_kernel, nb=nb, nc=nc, ns=ns)
    return pl.pallas_call(
        kern,
        grid=(nbd,),
        in_specs=[
            pl.BlockSpec((rows, LANES), lambda j: (0, j)),
            row(SSM_SW), row(SSM_SW), row(SSM_SW),
            mat(), mat(), mat(), mat(),
            row(gp),
            pl.BlockSpec((None, ns, half), lambda j: (l, 0, j)),
            pl.BlockSpec((None, ns, half), lambda j: (l, 0, j)),
        ],
        out_specs=[
            pl.BlockSpec((rows, LANES), lambda j: (0, j)),
            pl.BlockSpec((nb, half), lambda j: (0, j)),
            pl.BlockSpec((nb, half), lambda j: (0, j)),
            pl.BlockSpec((ns, half), lambda j: (0, j)),
            pl.BlockSpec((ns, half), lambda j: (0, j)),
        ],
        out_shape=[
            jax.ShapeDtypeStruct((rows, ch), F32),
            jax.ShapeDtypeStruct((nb, nbd * half), F32),
            jax.ShapeDtypeStruct((nb, nbd * half), F32),
            jax.ShapeDtypeStruct((ns, nbd * half), F32),
            jax.ShapeDtypeStruct((ns, nbd * half), F32),
        ],
        scratch_shapes=[
            pltpu.VMEM((SSM_T * LANES, SSM_SW), BF16),
            pltpu.VMEM((SSM_T * LANES, SSM_SW), BF16),
            pltpu.VMEM((SSM_T * LANES, 2 * LANES), BF16),
            pltpu.VMEM((rows_c, SSM_T * LANES), BF16),
            pltpu.VMEM((rows_c, SSM_SW), F32),
            pltpu.VMEM((rows_c, SSM_SW), F32),
            pltpu.VMEM((rows_c, SSM_SW), BF16),
        ],
        compiler_params=_cparams(("parallel",)),
        name="ssm",
    )(s_all, sw["are"], sw["aim"], sw["ldt"], sw["bre"], sw["bim"], sw["cre"], sw["cim"], sw["d"],
      state_re, state_im)


def _glu_norm_kernel(gy_ref, w_ref, bg_ref, o_ref):
    gy = gy_ref[...]
    z = jnp.dot(gy.astype(BF16), w_ref[...], preferred_element_type=F32)
    o_ref[...] = _rms_rows(gy * jax.nn.sigmoid(z), bg_ref[...]).astype(o_ref.dtype)


def _glu_norm(gy, w_glu, bg, l, *, tm):
    rows, ch = gy.shape
    return pl.pallas_call(
        _glu_norm_kernel,
        grid=(rows // tm,),
        in_specs=[
            pl.BlockSpec((tm, ch), lambda i: (i, 0)),
            pl.BlockSpec((None, ch, ch), lambda i: (l, 0, 0)),
            pl.BlockSpec((None, 1, ch), lambda i: (l, 0, 0)),
        ],
        out_specs=pl.BlockSpec((tm, ch), lambda i: (i, 0)),
        out_shape=jax.ShapeDtypeStruct((rows, ch), BF16),
        compiler_params=_cparams(("parallel",)),
        name="glu_norm",
    )(gy, w_glu, bg)


def _mix_out_kernel(c_ref, s_ref, wc_ref, ws_ref, x_ref, o_ref):
    acc = jnp.dot(c_ref[...], wc_ref[...], preferred_element_type=F32)
    acc = acc + jnp.dot(s_ref[...], ws_ref[...], preferred_element_type=F32)
    o_ref[...] = x_ref[...] + acc


def _mix_out(cn, sn, w_out, x, l, *, tm, tn):
    rows, d = x.shape
    ch = cn.shape[1]
    return pl.pallas_call(
        _mix_out_kernel,
        grid=(rows // tm, d // tn),
        in_specs=[
            pl.BlockSpec((tm, ch), lambda i, j: (i, 0)),
            pl.BlockSpec((tm, ch), lambda i, j: (i, 0)),
            pl.BlockSpec((None, ch, tn), lambda i, j: (l, 0, j)),
            pl.BlockSpec((None, ch, tn), lambda i, j: (l, 1, j)),
            pl.BlockSpec((tm, tn), lambda i, j: (i, j)),
        ],
        out_specs=pl.BlockSpec((tm, tn), lambda i, j: (i, j)),
        out_shape=jax.ShapeDtypeStruct((rows, d), F32),
        compiler_params=_cparams(("parallel", "arbitrary")),
        name="mix_out",
    )(cn, sn, w_out, w_out, x)


def _norm_proj_kernel(x_ref, gain_ref, *refs):
    nw = (len(refs) - 1) // 2
    w_refs, o_refs, hn_ref = refs[:nw], refs[nw:2 * nw], refs[2 * nw]
    _norm_rows_to(x_ref, gain_ref, hn_ref, x_ref.shape[0])
    hn = hn_ref[...]
    for w_ref, o_ref in zip(w_refs, o_refs):
        o_ref[...] = jnp.dot(hn, w_ref[...], preferred_element_type=F32)


def _norm_proj(x, gain, weights, l, *, tm, name, row0=0, nrows=None):
    d = x.shape[1]
    nrows = x.shape[0] - row0 if nrows is None else nrows
    n = weights[0].shape[2]
    blk0 = row0 // tm
    return pl.pallas_call(
        _norm_proj_kernel,
        grid=(nrows // tm,),
        in_specs=[pl.BlockSpec((tm, d), lambda i: (blk0 + i, 0)),
                  pl.BlockSpec((None, 1, d), lambda i: (l, 0, 0))]
                 + [pl.BlockSpec((None, d, n), lambda i: (l, 0, 0)) for _ in weights],
        out_specs=[pl.BlockSpec((tm, n), lambda i: (i, 0)) for _ in weights],
        out_shape=[jax.ShapeDtypeStruct((nrows, n), F32) for _ in weights],
        scratch_shapes=[pltpu.VMEM((tm, d), BF16)],
        compiler_params=_cparams(("parallel",)),
        name=name,
    )(x, gain, *weights)


def _proj_res_kernel(a_ref, w_ref, x_ref, o_ref):
    o_ref[...] = x_ref[...] + jnp.dot(a_ref[...].astype(BF16), w_ref[...],
                                      preferred_element_type=F32)


def _proj_res_rows(a, w, x, l, *, row0, tn, name):
    d = x.shape[1]
    m, k = a.shape
    blk0 = row0 // m
    return pl.pallas_call(
        _proj_res_kernel,
        grid=(d // tn,),
        in_specs=[
            pl.BlockSpec((m, k), lambda j: (0, 0)),
            pl.BlockSpec((None, k, tn), lambda j: (l, 0, j)),
            pl.BlockSpec((m, tn), lambda j: (blk0, j)),
        ],
        out_specs=pl.BlockSpec((m, tn), lambda j: (blk0, j)),
        out_shape=jax.ShapeDtypeStruct(x.shape, x.dtype),
        input_output_aliases={2: 0},
        compiler_params=_cparams(("parallel",)),
        name=name,
    )(a, w, x)


def _softmax_rows(s):
    p = jnp.exp(s - jnp.max(s, axis=-1, keepdims=True))
    return p / jnp.sum(p, axis=-1, keepdims=True)


def _attn_prompt_kernel(x_ref, gain_ref, wq_ref, k_ref, v_ref, wo_ref, o_ref, hn_ref, oc_ref,
                        *, heads, dh):
    _norm_rows_to(x_ref, gain_ref, hn_ref, x_ref.shape[0])
    q = jnp.dot(hn_ref[...], wq_ref[...], preferred_element_type=F32)
    scale = dh ** -0.5
    for h in range(heads):
        sl = slice(h * dh, (h + 1) * dh)
        s = _nt_dot(q[:, sl].astype(BF16), k_ref[:, sl].astype(BF16)) * scale
        w = _softmax_rows(s)
        oc_ref[:, sl] = jnp.dot(w.astype(BF16), v_ref[:, sl].astype(BF16),
                                preferred_element_type=F32).astype(BF16)
    o_ref[...] = x_ref[...] + jnp.dot(oc_ref[...], wo_ref[...], preferred_element_type=F32)


def _attn_prompt(x, gain, w_xq, k, v, w_xo, l, *, nb, seq, tq, heads):
    rows, d = x.shape
    width = k.shape[1]
    n_mem = k.shape[0] // nb
    per_seq = seq // tq
    kern = functools.partial(_attn_prompt_kernel, heads=heads, dh=width // heads)
    return pl.pallas_call(
        kern,
        grid=(nb, per_seq),
        in_specs=[
            pl.BlockSpec((tq, d), lambda b, i: (b * per_seq + i, 0)),
            pl.BlockSpec((None, 1, d), lambda b, i: (l, 0, 0)),
            pl.BlockSpec((None, d, width), lambda b, i: (l, 0, 0)),
            pl.BlockSpec((n_mem, width), lambda b, i: (b, 0)),
            pl.BlockSpec((n_mem, width), lambda b, i: (b, 0)),
            pl.BlockSpec((None, width, d), lambda b, i: (l, 0, 0)),
        ],
        out_specs=pl.BlockSpec((tq, d), lambda b, i: (b * per_seq + i, 0)),
        out_shape=jax.ShapeDtypeStruct((rows, d), F32),
        input_output_aliases={0: 0},
        scratch_shapes=[pltpu.VMEM((tq, d), BF16), pltpu.VMEM((tq, width), BF16)],
        compiler_params=_cparams(("parallel", "arbitrary")),
        name="attn_prompt",
    )(x, gain, w_xq, k, v, w_xo)


def _attn_sample_kernel(q_ref, k_ref, v_ref, o_ref, *, heads, dh):
    bs = q_ref.shape[0]
    nr = k_ref.shape[1]
    hp = 8
    q = q_ref[...]
    head_row = lax.broadcasted_iota(jnp.int32, (bs, hp, dh), 1)
    q4 = jnp.zeros((bs, hp, dh), F32)
    for h in range(heads):
        q4 = jnp.where(head_row == h, q[:, None, h * dh:(h + 1) * dh], q4)
    s = jnp.einsum("bhd,brd->bhr", q4.astype(BF16), k_ref[...].astype(BF16),
                   preferred_element_type=F32) * (dh ** -0.5)
    own = (lax.broadcasted_iota(jnp.int32, (bs, hp, nr), 2) % heads
           == lax.broadcasted_iota(jnp.int32, (bs, hp, nr), 1))
    w = _softmax_rows(jnp.where(own, s, -1e30))
    o = jnp.einsum("bhr,brd->bhd", w.astype(BF16), v_ref[...].astype(BF16),
                   preferred_element_type=F32)
    for h in range(heads):
        o_ref[:, h * dh:(h + 1) * dh] = o[:, h, :]


def _attn_sample(q, k_cache, v_cache, l, *, bs, heads):
    _, ns, nr, dh = k_cache.shape
    width = heads * dh
    kern = functools.partial(_attn_sample_kernel, heads=heads, dh=dh)
    return pl.pallas_call(
        kern,
        grid=(ns // bs,),
        in_specs=[
            pl.BlockSpec((bs, width), lambda j: (j, 0)),
            pl.BlockSpec((None, bs, nr, dh), lambda j: (l, j, 0, 0)),
            pl.BlockSpec((None, bs, nr, dh), lambda j: (l, j, 0, 0)),
        ],
        out_specs=pl.BlockSpec((bs, width), lambda j: (j, 0)),
        out_shape=jax.ShapeDtypeStruct((ns, width), F32),
        compiler_params=_cparams(("parallel",)),
        name="attn_sample",
    )(q, k_cache, v_cache)


def _ffn_kernel(x_hbm_ref, gain_ref, wu_ref, wd_ref, o_ref, hn_ref, sem):
    tm = o_ref.shape[0]

    @pl.when(pl.program_id(1) == 0)
    def _():
        r0 = pl.multiple_of(pl.program_id(0) * tm, tm)
        cp = pltpu.make_async_copy(x_hbm_ref.at[pl.ds(r0, tm), :], o_ref, sem)
        cp.start()
        cp.wait()
        _norm_rows_to(o_ref, gain_ref, hn_ref, tm)

    wu = wu_ref[...].astype(BF16)
    hm = tm // 2
    h = jnp.concatenate(
        [jnp.dot(hn_ref[0:hm, :], wu, preferred_element_type=F32),
         jnp.dot(hn_ref[hm:tm, :], wu, preferred_element_type=F32)], axis=0)
    h = jnp.square(jnp.maximum(h, 0.0)).astype(BF16)
    o_ref[...] += jnp.dot(h, wd_ref[...].astype(BF16), preferred_element_type=F32)


def _ffn(x, gain, w_up, w_down, l, *, tm, tf):
    rows, d = x.shape
    dff = w_up.shape[2]
    return pl.pallas_call(
        _ffn_kernel,
        grid=(rows // tm, dff // tf),
        in_specs=[
            pl.BlockSpec(memory_space=pl.ANY),
            pl.BlockSpec((None, 1, d), lambda i, k: (l, 0, 0)),
            pl.BlockSpec((None, d, tf), lambda i, k: (l, 0, k)),
            pl.BlockSpec((None, tf, d), lambda i, k: (l, k, 0)),
        ],
        out_specs=pl.BlockSpec((tm, d), lambda i, k: (i, 0)),
        out_shape=jax.ShapeDtypeStruct((rows, d), F32),
        scratch_shapes=[pltpu.VMEM((tm, d), BF16), pltpu.SemaphoreType.DMA(())],
        compiler_params=pltpu.CompilerParams(
            dimension_semantics=("arbitrary", "arbitrary"),
            vmem_limit_bytes=V7X_FFN_VMEM_LIMIT_BYTES),
        name="ffn",
    )(x, gain, w_up, w_down)


def _final_norm_kernel(x_ref, gain_ref, o_ref):
    o_ref[...] = _rms_rows(x_ref[...], gain_ref[...])


def _final_norm(x, gain, *, row0, nrows, tm):
    d = x.shape[1]
    return pl.pallas_call(
        _final_norm_kernel,
        grid=(nrows // tm,),
        in_specs=[pl.BlockSpec((tm, d), lambda i: (row0 // tm + i, 0)),
                  pl.BlockSpec((1, d), lambda i: (0, 0))],
        out_specs=pl.BlockSpec((tm, d), lambda i: (i, 0)),
        out_shape=jax.ShapeDtypeStruct((nrows, d), F32),
        compiler_params=_cparams(("parallel",)),
        name="final_norm",
    )(x, gain)


def kernel(x_prompt, x_sample, mem_prompt, cache_conv, state_ssm_re, state_ssm_im, cache_mem_k, cache_mem_v, norm_mix_g, w_in, conv_w, conv_b, conv_ln_g, conv_ln_b, ssm_a_re, ssm_a_im, ssm_log_dt, ssm_b_re, ssm_b_im, ssm_c_re, ssm_c_im, ssm_d, w_glu, branch_g_conv, branch_g_ssm, w_out, norm_x_g, norm_mem_g, w_xq, w_xk, w_xv, w_xo, norm_ffn_g, w_up, w_down, norm_final_g):
    nb, seq, d = x_prompt.shape
    ns = x_sample.shape[0]
    depth = w_in.shape[0]
    n_mem = mem_prompt.shape[1]
    heads, dh = cache_mem_k.shape[3], cache_mem_k.shape[4]
    width = heads * dh
    ch = conv_w.shape[2]
    g_ssm, n_state = ssm_a_re.shape[1], ssm_a_re.shape[2]
    rows_p = nb * seq
    rows = rows_p + ns

    x = jnp.concatenate([x_prompt.reshape(rows_p, d), x_sample.reshape(ns, d)], axis=0)
    mem = mem_prompt.reshape(nb * n_mem, d)
    vec = lambda a: a.reshape(depth, 1, a.shape[-1])
    mix_g, x_g, mem_g, ffn_g = vec(norm_mix_g), vec(norm_x_g), vec(norm_mem_g), vec(norm_ffn_g)
    cb, lng, lnb, bgc, bgs = (vec(conv_b), vec(conv_ln_g), vec(conv_ln_b),
                              vec(branch_g_conv), vec(branch_g_ssm))
    w_in_b, w_glu_b, w_out_b = w_in.astype(BF16), w_glu.astype(BF16), w_out.astype(BF16)
    w_xq_b, w_xk_b, w_xv_b, w_xo_b = (w_xq.astype(BF16), w_xk.astype(BF16),
                                      w_xv.astype(BF16), w_xo.astype(BF16))
    sw = _ssm_weights(ssm_a_re, ssm_a_im, ssm_log_dt, ssm_b_re, ssm_b_im, ssm_c_re, ssm_c_im, ssm_d)
    st_re = state_ssm_re.reshape(depth, ns, g_ssm * n_state)
    st_im = state_ssm_im.reshape(depth, ns, g_ssm * n_state)
    k_cache = cache_mem_k.reshape(depth, ns, n_mem * heads, dh)
    v_cache = cache_mem_v.reshape(depth, ns, n_mem * heads, dh)
    cache_t = jnp.transpose(cache_conv, (0, 2, 1, 3))

    conv_wb = jnp.broadcast_to(conv_w[:, :, None, :], (depth, CONV_W, 8, ch))
    tm = _pick_tile(rows, 640, NORM_CHUNK)
    cache_out = jnp.zeros(cache_t.shape, F32)
    cn = jnp.zeros((rows, ch), BF16)
    mk_p, mv_p, cb_p, sr_p, si_p, sr_s, si_s = [], [], [], [], [], [], []
    for l in range(depth):
        g_all, s_all = _in_proj(x, mix_g, w_in_b, l, tm=tm, tn=_pick_tile(ch, 256, LANES))
        cn = _conv_prompt(g_all, conv_wb, cb, lng, lnb, bgc, cn, l, nb=nb, seq=seq)
        cache_out, cn = _conv_sample(cache_t, g_all, conv_w, cb, lng, lnb, bgc, cn, cache_out, l,
                                     row0=rows_p, bs=16)
        gy, hre_p, him_p, hre_s, him_s = _ssm(s_all, sw, st_re, st_im, l, nb=nb, seq=seq, ns=ns)
        sn = _glu_norm(gy, w_glu_b, bgs, l, tm=_pick_tile(rows, 320, NORM_CHUNK))
        x = _mix_out(cn, sn, w_out_b, x, l, tm=tm, tn=_pick_tile(d, 1024, LANES))
        k_p, v_p = _norm_proj(mem, mem_g, [w_xk_b, w_xv_b], l,
                              tm=_pick_tile(nb * n_mem, 512, NORM_CHUNK), name="mem_kv")
        x = _attn_prompt(x, x_g, w_xq_b, k_p, v_p, w_xo_b, l, nb=nb, seq=seq,
                         tq=_pick_tile(seq, 256, NORM_CHUNK), heads=heads)
        (q_s,) = _norm_proj(x, x_g, [w_xq_b], l, tm=ns, name="q_proj_sample",
                            row0=rows_p, nrows=ns)
        o_s = _attn_sample(q_s, k_cache, v_cache, l, bs=8, heads=heads)
        x = _proj_res_rows(o_s, w_xo_b, x, l, row0=rows_p,
                           tn=_pick_tile(d, 2048, LANES), name="attn_out_sample")
        x = _ffn(x, ffn_g, w_up, w_down, l, tm=_pick_tile(rows, 832, NORM_CHUNK),
                 tf=_pick_tile(w_up.shape[2], 256, LANES))

        mk_p.append(k_p.reshape(nb, n_mem, heads, dh))
        mv_p.append(v_p.reshape(nb, n_mem, heads, dh))
        cb_p.append(jnp.stack([g_all[(b + 1) * seq - (CONV_W - 1):(b + 1) * seq]
                               for b in range(nb)]))
        sr_p.append(hre_p.reshape(nb, g_ssm, n_state))
        si_p.append(him_p.reshape(nb, g_ssm, n_state))
        sr_s.append(hre_s.reshape(ns, g_ssm, n_state))
        si_s.append(him_s.reshape(ns, g_ssm, n_state))

    gain = norm_final_g.reshape(1, d)
    y_prompt = _final_norm(x, gain, row0=0, nrows=rows_p, tm=512).reshape(nb, seq, d)
    y_sample = _final_norm(x, gain, row0=rows_p, nrows=ns, tm=ns).reshape(ns, 1, d)
    return (y_prompt, y_sample, jnp.stack(mk_p), jnp.stack(mv_p), jnp.stack(cb_p),
            jnp.stack(sr_p), jnp.stack(si_p), jnp.transpose(cache_out, (0, 2, 1, 3)),
            jnp.stack(sr_s), jnp.stack(si_s))
```

```python
import functools

import jax
import jax.numpy as jnp
from jax import lax
from jax.experimental import pallas as pl
from jax.experimental.pallas import tpu as pltpu

F32 = jnp.float32
BF16 = jnp.bfloat16
EPS = 1e-6

V7X_VMEM_LIMIT_BYTES = 56 * 1024 * 1024
V7X_FFN_VMEM_LIMIT_BYTES = 60 * 1024 * 1024
LANES = 128
NORM_CHUNK = 32
FFN_ROW_CHUNK = 64

CONV_W = 31
CONV_HALO = 32
CONV_T = 128

SSM_T = 16
SSM_GB = 8
SSM_P = 16
SSM_N = 64
SSM_SW = 2 * SSM_GB * SSM_N


def _cparams(sem):
    return pltpu.CompilerParams(dimension_semantics=sem,
                                vmem_limit_bytes=V7X_VMEM_LIMIT_BYTES)


def _rms_rows(x, g):
    ms = jnp.mean(x * x, axis=-1, keepdims=True)
    return x * lax.rsqrt(ms + EPS) * g


def _norm_rows_to(x_ref, gain_ref, hn_ref, rows):
    gain = gain_ref[...]
    chunk = next(c for c in (4 * NORM_CHUNK, 2 * NORM_CHUNK, NORM_CHUNK) if rows % c == 0)

    def body(c, carry):
        r = pl.multiple_of(c * chunk, chunk)
        hn_ref[pl.ds(r, chunk), :] = _rms_rows(
            x_ref[pl.ds(r, chunk), :], gain).astype(hn_ref.dtype)
        return carry

    lax.fori_loop(0, rows // chunk, body, 0)


def _pick_tile(n, target, mult):
    best = None
    for t in range(mult, min(n, target) + 1, mult):
        if n % t == 0:
            best = t
    if best is None:
        raise ValueError(f"no tile for extent {n} (multiple of {mult}, <= {target})")
    return best


def _nt_dot(a, b):
    return lax.dot_general(a, b, (((1,), (1,)), ((), ())),
                           preferred_element_type=F32)


def _in_proj_kernel(x_ref, gain_ref, wv_ref, wg_ref, ws_ref, g_ref, s_ref, hn_ref):
    @pl.when(pl.program_id(1) == 0)
    def _():
        _norm_rows_to(x_ref, gain_ref, hn_ref, x_ref.shape[0])

    hn = hn_ref[...]
    val = jnp.dot(hn, wv_ref[...], preferred_element_type=F32)
    gate = jnp.dot(hn, wg_ref[...], preferred_element_type=F32)
    g_ref[...] = val * jax.nn.sigmoid(gate)
    s_ref[...] = jnp.dot(hn, ws_ref[...], preferred_element_type=F32)


def _in_proj(x, gain, w_in, l, *, tm, tn):
    rows, d = x.shape
    ch = w_in.shape[2] // 3
    nj = ch // tn
    return pl.pallas_call(
        _in_proj_kernel,
        grid=(rows // tm, nj),
        in_specs=[
            pl.BlockSpec((tm, d), lambda i, j: (i, 0)),
            pl.BlockSpec((None, 1, d), lambda i, j: (l, 0, 0)),
            pl.BlockSpec((None, d, tn), lambda i, j: (l, 0, j)),
            pl.BlockSpec((None, d, tn), lambda i, j: (l, 0, nj + j)),
            pl.BlockSpec((None, d, tn), lambda i, j: (l, 0, 2 * nj + j)),
        ],
        out_specs=[
            pl.BlockSpec((tm, tn), lambda i, j: (i, j)),
            pl.BlockSpec((tm, tn), lambda i, j: (i, j)),
        ],
        out_shape=[jax.ShapeDtypeStruct((rows, ch), F32),
                   jax.ShapeDtypeStruct((rows, ch), F32)],
        scratch_shapes=[pltpu.VMEM((tm, d), BF16)],
        compiler_params=_cparams(("parallel", "arbitrary")),
        name="in_proj",
    )(x, gain, w_in, w_in, w_in)


def _conv_tail(c, b_ref, lng_ref, lnb_ref, bg_ref):
    c = c + b_ref[...]
    xc = c - jnp.mean(c, axis=-1, keepdims=True)
    var = jnp.mean(xc * xc, axis=-1, keepdims=True)
    y = xc * lax.rsqrt(var + EPS) * lng_ref[...] + lnb_ref[...]
    y = y * jax.nn.sigmoid(y)
    return _rms_rows(y, bg_ref[...])


def _conv_prompt_kernel(cur_ref, prev_ref, wb_ref, b_ref, lng_ref, lnb_ref, bg_ref, dst_any_ref,
                        o_ref, ext_ref, c_ref):
    del dst_any_ref
    t, ch = cur_ref.shape
    rows_e = CONV_HALO + t
    first = pl.program_id(1) == 0
    ext_ref[0, 0:CONV_HALO, :] = jnp.where(first, 0.0, prev_ref[...])
    ext_ref[0, CONV_HALO:rows_e, :] = cur_ref[...]
    for r in range(1, 8):
        ext_ref[r] = pltpu.roll(ext_ref[0], rows_e - r, axis=0)
    off = CONV_HALO - (CONV_W - 1)
    n_rb = t // 8
    for lc in range(ch // LANES):
        lanes = slice(lc * LANES, (lc + 1) * LANES)
        acc = [None] * n_rb
        for r in range(8):
            taps = [(k, (off + k) // 8) for k in range(CONV_W) if (off + k) % 8 == r]
            w = {k: wb_ref[k, :, lanes] for k, _ in taps}
            for m in range(rows_e // 8):
                uses = [(k, m - a) for k, a in taps if 0 <= m - a < n_rb]
                if not uses:
                    continue
                v = ext_ref[r, m * 8:(m + 1) * 8, lanes]
                for k, rb in uses:
                    term = w[k] * v
                    acc[rb] = term if acc[rb] is None else acc[rb] + term
        for rb in range(n_rb):
            c_ref[rb * 8:(rb + 1) * 8, lanes] = acc[rb]
    o_ref[...] = _conv_tail(c_ref[...], b_ref, lng_ref, lnb_ref, bg_ref).astype(o_ref.dtype)


def _conv_prompt(g_all, conv_wb, conv_b, ln_g, ln_b, bg, dst, l, *, nb, seq):
    rows, ch = g_all.shape
    t = CONV_T
    per_seq = seq // t
    vec = lambda: pl.BlockSpec((None, 1, ch), lambda b, i: (l, 0, 0))
    return pl.pallas_call(
        _conv_prompt_kernel,
        grid=(nb, per_seq),
        in_specs=[
            pl.BlockSpec((t, ch), lambda b, i: (b * per_seq + i, 0)),
            pl.BlockSpec((CONV_HALO, ch),
                         lambda b, i: (jnp.maximum((b * seq + i * t) // CONV_HALO - 1, 0), 0)),
            pl.BlockSpec((None, CONV_W, 8, ch), lambda b, i: (l, 0, 0, 0)),
            vec(), vec(), vec(), vec(),
            pl.BlockSpec(memory_space=pl.ANY),
        ],
        out_specs=pl.BlockSpec((t, ch), lambda b, i: (b * per_seq + i, 0)),
        out_shape=jax.ShapeDtypeStruct(dst.shape, dst.dtype),
        input_output_aliases={7: 0},
        scratch_shapes=[pltpu.VMEM((8, CONV_HALO + t, ch), F32), pltpu.VMEM((t, ch), F32)],
        compiler_params=_cparams(("parallel", "arbitrary")),
        name="conv_prompt",
    )(g_all, g_all, conv_wb, conv_b, ln_g, ln_b, bg, dst)


def _conv_sample_kernel(cache_ref, g_ref, w_ref, b_ref, lng_ref, lnb_ref, bg_ref, *refs):
    cache_out_ref, cn_ref = refs[-2:]
    hist = CONV_W - 1
    g = g_ref[...]
    c = w_ref[hist:CONV_W, :] * g
    for k in range(hist):
        c = c + w_ref[k:k + 1, :] * cache_ref[k]
    cn_ref[...] = _conv_tail(c, b_ref, lng_ref, lnb_ref, bg_ref).astype(cn_ref.dtype)
    cache_out_ref[0:hist - 1] = cache_ref[1:hist]
    cache_out_ref[hist - 1] = g


def _conv_sample(cache_t, g_all, conv_w, conv_b, ln_g, ln_b, bg, cn_all, cache_out, l, *, row0, bs):
    _, hist, ns, ch = cache_t.shape
    vec = lambda: pl.BlockSpec((None, 1, ch), lambda j: (l, 0, 0))
    return pl.pallas_call(
        _conv_sample_kernel,
        grid=(ns // bs,),
        in_specs=[
            pl.BlockSpec((None, hist, bs, ch), lambda j: (l, 0, j, 0)),
            pl.BlockSpec((bs, ch), lambda j: (row0 // bs + j, 0)),
            pl.BlockSpec((None, CONV_W, ch), lambda j: (l, 0, 0)),
            vec(), vec(), vec(), vec(),
            pl.BlockSpec(memory_space=pl.ANY),
            pl.BlockSpec(memory_space=pl.ANY),
        ],
        out_specs=[
            pl.BlockSpec((None, hist, bs, ch), lambda j: (l, 0, j, 0)),
            pl.BlockSpec((bs, ch), lambda j: (row0 // bs + j, 0)),
        ],
        out_shape=[jax.ShapeDtypeStruct(cache_t.shape, F32),
                   jax.ShapeDtypeStruct(cn_all.shape, cn_all.dtype)],
        input_output_aliases={7: 1, 8: 0},
        compiler_params=_cparams(("parallel",)),
        name="conv_sample",
    )(cache_t, g_all, conv_w, conv_b, ln_g, ln_b, bg, cn_all, cache_out)


def _ssm_kernel(u_ref, are_ref, aim_ref, ldt_ref, bre_ref, bim_ref, cre_ref, cim_ref, d_ref,
                hre_in_ref, him_in_ref,
                gy_ref, hre_p_ref, him_p_ref, hre_s_ref, him_s_ref,
                ws_ref, wit_ref, dst_ref, x_ref, s_ref, hp_ref, hpb_ref, *, nb, nc, ns):
    t_len = SSM_T
    half = SSM_SW // 2
    rows_c = nb * nc
    row_s = rows_c * t_len

    a_re = are_ref[...]
    a_im = aim_ref[...]
    dt = jnp.exp(ldt_ref[...])
    mag = jnp.exp(a_re * dt)
    ang = a_im * dt
    abar_r = mag * jnp.cos(ang)
    abar_i = mag * jnp.sin(ang)
    nr = abar_r - 1.0
    ni = abar_i
    den = a_re * a_re + a_im * a_im
    cf_r = (nr * a_re + ni * a_im) / den
    cf_i = (ni * a_re - nr * a_im) / den

    pw = [(jnp.ones_like(abar_r), jnp.zeros_like(abar_r))]
    for _ in range(t_len):
        pr, pi = pw[-1]
        pw.append((pr * abar_r - pi * abar_i, pr * abar_i + pi * abar_r))

    gp = bre_ref.shape[0]
    own = (lax.broadcasted_iota(jnp.int32, (gp, half), 0) // SSM_P
           == lax.broadcasted_iota(jnp.int32, (gp, half), 1) // SSM_N)

    def expand(m_ref):
        wide = jnp.concatenate([m_ref[...]] * (half // LANES), axis=1)
        return jnp.where(own, wide, 0.0)

    bre, bim, cre, cim = expand(bre_ref), expand(bim_ref), expand(cre_ref), expand(cim_ref)
    b1 = jnp.concatenate([bre, bim], axis=1)
    b2 = jnp.concatenate([-bim, bre], axis=1)
    c1 = jnp.concatenate([cre, -cim], axis=1)
    c2 = jnp.concatenate([-cim, -cre], axis=1)
    c1_hi = c1.astype(BF16)
    c1_lo = (c1 - c1_hi.astype(F32)).astype(BF16)
    dst_ref[(t_len - 1) * LANES:, 0:LANES] = jnp.zeros((LANES, LANES), BF16)
    for t in range(t_len):
        sl = slice(t * LANES, (t + 1) * LANES)
        pr, pi = pw[t_len - 1 - t]
        acr = pr * cf_r - pi * cf_i
        aci = pr * cf_i + pi * cf_r
        ws_t = acr * b1 + aci * b2
        ws_hi = ws_t.astype(BF16)
        ws_lo = (ws_t - ws_hi.astype(F32)).astype(BF16)
        ws_ref[sl, :] = ws_hi
        lag = (_nt_dot(ws_hi, c1_hi) + _nt_dot(ws_hi, c1_lo) + _nt_dot(ws_lo, c1_hi)).astype(BF16)
        dst_ref[sl, LANES:] = lag
        if t > 0:
            dst_ref[(t - 1) * LANES:t * LANES, 0:LANES] = lag
        pr, pi = pw[t + 1]
        wit_ref[sl, :] = (pr * c1 + pi * c2).astype(BF16)
        x_ref[:, sl] = u_ref[pl.ds(t, rows_c, stride=t_len), :].astype(BF16)

    s_ref[...] = jnp.dot(x_ref[...], ws_ref[...], preferred_element_type=F32)
    at_r = pw[t_len][0][:, :half]
    at_i = pw[t_len][1][:, :half]

    def scan_body(c, carry):
        new = []
        for b in range(nb):
            hr, hi = carry[b]
            row = pl.ds(b * nc + c, 1)
            hp_ref[row, 0:half] = hr
            hp_ref[row, half:] = hi
            sr = s_ref[row, 0:half]
            si = s_ref[row, half:]
            new.append((at_r * hr - at_i * hi + sr, at_r * hi + at_i * hr + si))
        return tuple(new)

    zero = jnp.zeros((1, half), F32)
    final = lax.fori_loop(0, nc, scan_body, tuple((zero, zero) for _ in range(nb)))
    for b in range(nb):
        hre_p_ref[b:b + 1, :] = final[b][0]
        him_p_ref[b:b + 1, :] = final[b][1]
    hpb_ref[...] = hp_ref[...].astype(BF16)

    d_skip = d_ref[...]
    for t in range(0, t_len, 2):
        y = jnp.dot(x_ref[:, 0:(t + 2) * LANES], dst_ref[(t_len - 2 - t) * LANES:, :],
                    preferred_element_type=F32)
        y = y + _nt_dot(hpb_ref[...], wit_ref[t * LANES:(t + 2) * LANES, :])
        for q in range(2):
            u_t = u_ref[pl.ds(t + q, rows_c, stride=t_len), :]
            gy_ref[pl.ds(t + q, rows_c, stride=t_len), :] = jax.nn.gelu(
                y[:, q * LANES:(q + 1) * LANES] + d_skip * u_t)

    u_s = u_ref[row_s:row_s + ns, :]
    bu = jnp.dot(u_s.astype(BF16), ws_ref[(t_len - 1) * LANES:, :], preferred_element_type=F32)
    a1_r = abar_r[:, :half]
    a1_i = abar_i[:, :half]
    hre = hre_in_ref[...]
    him = him_in_ref[...]
    nre = a1_r * hre - a1_i * him + bu[:, :half]
    nim = a1_r * him + a1_i * hre + bu[:, half:]
    hre_s_ref[...] = nre
    him_s_ref[...] = nim
    h_cat = jnp.concatenate([nre, nim], axis=1).astype(BF16)
    y_s = _nt_dot(h_cat, c1.astype(BF16))
    gy_ref[row_s:row_s + ns, :] = jax.nn.gelu(y_s + d_skip * u_s)


def _ssm_weights(a_re, a_im, log_dt, b_re, b_im, c_re, c_im, ssm_d):
    nl, g, n = a_re.shape
    p = b_re.shape[3]
    nbd = g // SSM_GB
    assert (n, p, 2 * n) == (SSM_N, SSM_P, LANES)

    def rowvec(a):
        a = a.reshape(nl, nbd, 1, SSM_GB * n)
        return jnp.concatenate([a, a], axis=-1)

    def rows_gp(m):
        m = m.reshape(nl, nbd, SSM_GB * p, n)
        return jnp.concatenate([m, m], axis=-1)

    return dict(
        are=rowvec(a_re), aim=rowvec(a_im),
        ldt=rowvec(jnp.broadcast_to(log_dt[:, :, None], (nl, g, n))),
        bre=rows_gp(b_re.reshape(nl, nbd, SSM_GB, n, p).transpose(0, 1, 2, 4, 3)),
        bim=rows_gp(b_im.reshape(nl, nbd, SSM_GB, n, p).transpose(0, 1, 2, 4, 3)),
        cre=rows_gp(c_re.reshape(nl, nbd, SSM_GB, p, n)),
        cim=rows_gp(c_im.reshape(nl, nbd, SSM_GB, p, n)),
        d=ssm_d.reshape(nl, nbd, 1, SSM_GB * p),
    )


def _ssm(s_all, sw, state_re, state_im, l, *, nb, seq, ns):
    rows, ch = s_all.shape
    nbd = ch // LANES
    nc = seq // SSM_T
    rows_c = nb * nc
    half = SSM_SW // 2
    gp = SSM_GB * SSM_P
    row = lambda w: pl.BlockSpec((None, None, 1, w), lambda j: (l, j, 0, 0))
    mat = lambda: pl.BlockSpec((None, None, gp, LANES), lambda j: (l, j, 0, 0))
    kern = functools.partial(_ssm_kernel, nb=nb, nc=nc, ns=ns)
    return pl.pallas_call(
        kern,
        grid=(nbd,),
        in_specs=[
            pl.BlockSpec((rows, LANES), lambda j: (0, j)),
            row(SSM_SW), row(SSM_SW), row(SSM_SW),
            mat(), mat(), mat(), mat(),
            row(gp),
            pl.BlockSpec((None, ns, half), lambda j: (l, 0, j)),
            pl.BlockSpec((None, ns, half), lambda j: (l, 0, j)),
        ],
        out_specs=[
            pl.BlockSpec((rows, LANES), lambda j: (0, j)),
            pl.BlockSpec((nb, half), lambda j: (0, j)),
            pl.BlockSpec((nb, half), lambda j: (0, j)),
            pl.BlockSpec((ns, half), lambda j: (0, j)),
            pl.BlockSpec((ns, half), lambda j: (0, j)),
        ],
        out_shape=[
            jax.ShapeDtypeStruct((rows, ch), F32),
            jax.ShapeDtypeStruct((nb, nbd * half), F32),
            jax.ShapeDtypeStruct((nb, nbd * half), F32),
            jax.ShapeDtypeStruct((ns, nbd * half), F32),
            jax.ShapeDtypeStruct((ns, nbd * half), F32),
        ],
        scratch_shapes=[
            pltpu.VMEM((SSM_T * LANES, SSM_SW), BF16),
            pltpu.VMEM((SSM_T * LANES, SSM_SW), BF16),
            pltpu.VMEM((SSM_T * LANES, 2 * LANES), BF16),
            pltpu.VMEM((rows_c, SSM_T * LANES), BF16),
            pltpu.VMEM((rows_c, SSM_SW), F32),
            pltpu.VMEM((rows_c, SSM_SW), F32),
            pltpu.VMEM((rows_c, SSM_SW), BF16),
        ],
        compiler_params=_cparams(("parallel",)),
        name="ssm",
    )(s_all, sw["are"], sw["aim"], sw["ldt"], sw["bre"], sw["bim"], sw["cre"], sw["cim"], sw["d"],
      state_re, state_im)


def _glu_norm_kernel(gy_ref, w_ref, bg_ref, o_ref):
    gy = gy_ref[...]
    z = jnp.dot(gy.astype(BF16), w_ref[...], preferred_element_type=F32)
    o_ref[...] = _rms_rows(gy * jax.nn.sigmoid(z), bg_ref[...]).astype(o_ref.dtype)


def _glu_norm(gy, w_glu, bg, l, *, tm):
    rows, ch = gy.shape
    return pl.pallas_call(
        _glu_norm_kernel,
        grid=(rows // tm,),
        in_specs=[
            pl.BlockSpec((tm, ch), lambda i: (i, 0)),
            pl.BlockSpec((None, ch, ch), lambda i: (l, 0, 0)),
            pl.BlockSpec((None, 1, ch), lambda i: (l, 0, 0)),
        ],
        out_specs=pl.BlockSpec((tm, ch), lambda i: (i, 0)),
        out_shape=jax.ShapeDtypeStruct((rows, ch), BF16),
        compiler_params=_cparams(("parallel",)),
        name="glu_norm",
    )(gy, w_glu, bg)


def _mix_out_kernel(c_ref, s_ref, wc_ref, ws_ref, x_ref, o_ref):
    acc = jnp.dot(c_ref[...], wc_ref[...], preferred_element_type=F32)
    acc = acc + jnp.dot(s_ref[...], ws_ref[...], preferred_element_type=F32)
    o_ref[...] = x_ref[...] + acc


def _mix_out(cn, sn, w_out, x, l, *, tm, tn):
    rows, d = x.shape
    ch = cn.shape[1]
    return pl.pallas_call(
        _mix_out_kernel,
        grid=(rows // tm, d // tn),
        in_specs=[
            pl.BlockSpec((tm, ch), lambda i, j: (i, 0)),
            pl.BlockSpec((tm, ch), lambda i, j: (i, 0)),
            pl.BlockSpec((None, ch, tn), lambda i, j: (l, 0, j)),
            pl.BlockSpec((None, ch, tn), lambda i, j: (l, 1, j)),
            pl.BlockSpec((tm, tn), lambda i, j: (i, j)),
        ],
        out_specs=pl.BlockSpec((tm, tn), lambda i, j: (i, j)),
        out_shape=jax.ShapeDtypeStruct((rows, d), F32),
        compiler_params=_cparams(("parallel", "arbitrary")),
        name="mix_out",
    )(cn, sn, w_out, w_out, x)


def _norm_proj_kernel(x_ref, gain_ref, *refs):
    nw = (len(refs) - 1) // 2
    w_refs, o_refs, hn_ref = refs[:nw], refs[nw:2 * nw], refs[2 * nw]
    _norm_rows_to(x_ref, gain_ref, hn_ref, x_ref.shape[0])
    hn = hn_ref[...]
    for w_ref, o_ref in zip(w_refs, o_refs):
        o_ref[...] = jnp.dot(hn, w_ref[...], preferred_element_type=F32)


def _norm_proj(x, gain, weights, l, *, tm, name, row0=0, nrows=None):
    d = x.shape[1]
    nrows = x.shape[0] - row0 if nrows is None else nrows
    n = weights[0].shape[2]
    blk0 = row0 // tm
    return pl.pallas_call(
        _norm_proj_kernel,
        grid=(nrows // tm,),
        in_specs=[pl.BlockSpec((tm, d), lambda i: (blk0 + i, 0)),
                  pl.BlockSpec((None, 1, d), lambda i: (l, 0, 0))]
                 + [pl.BlockSpec((None, d, n), lambda i: (l, 0, 0)) for _ in weights],
        out_specs=[pl.BlockSpec((tm, n), lambda i: (i, 0)) for _ in weights],
        out_shape=[jax.ShapeDtypeStruct((nrows, n), F32) for _ in weights],
        scratch_shapes=[pltpu.VMEM((tm, d), BF16)],
        compiler_params=_cparams(("parallel",)),
        name=name,
    )(x, gain, *weights)


def _proj_res_kernel(a_ref, w_ref, x_ref, o_ref):
    o_ref[...] = x_ref[...] + jnp.dot(a_ref[...].astype(BF16), w_ref[...],
                                      preferred_element_type=F32)


def _proj_res_rows(a, w, x, l, *, row0, tn, name):
    d = x.shape[1]
    m, k = a.shape
    blk0 = row0 // m
    return pl.pallas_call(
        _proj_res_kernel,
        grid=(d // tn,),
        in_specs=[
            pl.BlockSpec((m, k), lambda j: (0, 0)),
            pl.BlockSpec((None, k, tn), lambda j: (l, 0, j)),
            pl.BlockSpec((m, tn), lambda j: (blk0, j)),
        ],
        out_specs=pl.BlockSpec((m, tn), lambda j: (blk0, j)),
        out_shape=jax.ShapeDtypeStruct(x.shape, x.dtype),
        input_output_aliases={2: 0},
        compiler_params=_cparams(("parallel",)),
        name=name,
    )(a, w, x)


def _softmax_rows(s):
    p = jnp.exp(s - jnp.max(s, axis=-1, keepdims=True))
    return p / jnp.sum(p, axis=-1, keepdims=True)


def _attn_prompt_kernel(x_ref, gain_ref, wq_ref, k_ref, v_ref, wo_ref, o_ref, hn_ref, oc_ref,
                        *, heads, dh):
    _norm_rows_to(x_ref, gain_ref, hn_ref, x_ref.shape[0])
    q = jnp.dot(hn_ref[...], wq_ref[...], preferred_element_type=F32)
    scale = dh ** -0.5
    for h in range(heads):
        sl = slice(h * dh, (h + 1) * dh)
        s = _nt_dot(q[:, sl].astype(BF16), k_ref[:, sl].astype(BF16)) * scale
        w = _softmax_rows(s)
        oc_ref[:, sl] = jnp.dot(w.astype(BF16), v_ref[:, sl].astype(BF16),
                                preferred_element_type=F32).astype(BF16)
    o_ref[...] = x_ref[...] + jnp.dot(oc_ref[...], wo_ref[...], preferred_element_type=F32)


def _attn_prompt(x, gain, w_xq, k, v, w_xo, l, *, nb, seq, tq, heads):
    rows, d = x.shape
    width = k.shape[1]
    n_mem = k.shape[0] // nb
    per_seq = seq // tq
    kern = functools.partial(_attn_prompt_kernel, heads=heads, dh=width // heads)
    return pl.pallas_call(
        kern,
        grid=(nb, per_seq),
        in_specs=[
            pl.BlockSpec((tq, d), lambda b, i: (b * per_seq + i, 0)),
            pl.BlockSpec((None, 1, d), lambda b, i: (l, 0, 0)),
            pl.BlockSpec((None, d, width), lambda b, i: (l, 0, 0)),
            pl.BlockSpec((n_mem, width), lambda b, i: (b, 0)),
            pl.BlockSpec((n_mem, width), lambda b, i: (b, 0)),
            pl.BlockSpec((None, width, d), lambda b, i: (l, 0, 0)),
        ],
        out_specs=pl.BlockSpec((tq, d), lambda b, i: (b * per_seq + i, 0)),
        out_shape=jax.ShapeDtypeStruct((rows, d), F32),
        input_output_aliases={0: 0},
        scratch_shapes=[pltpu.VMEM((tq, d), BF16), pltpu.VMEM((tq, width), BF16)],
        compiler_params=_cparams(("parallel", "arbitrary")),
        name="attn_prompt",
    )(x, gain, w_xq, k, v, w_xo)


def _attn_sample_kernel(q_ref, k_ref, v_ref, o_ref, *, heads, dh):
    bs = q_ref.shape[0]
    nr = k_ref.shape[1]
    hp = 8
    q = q_ref[...]
    head_row = lax.broadcasted_iota(jnp.int32, (bs, hp, dh), 1)
    q4 = jnp.zeros((bs, hp, dh), F32)
    for h in range(heads):
        q4 = jnp.where(head_row == h, q[:, None, h * dh:(h + 1) * dh], q4)
    s = jnp.einsum("bhd,brd->bhr", q4.astype(BF16), k_ref[...].astype(BF16),
                   preferred_element_type=F32) * (dh ** -0.5)
    own = (lax.broadcasted_iota(jnp.int32, (bs, hp, nr), 2) % heads
           == lax.broadcasted_iota(jnp.int32, (bs, hp, nr), 1))
    w = _softmax_rows(jnp.where(own, s, -1e30))
    o = jnp.einsum("bhr,brd->bhd", w.astype(BF16), v_ref[...].astype(BF16),
                   preferred_element_type=F32)
    for h in range(heads):
        o_ref[:, h * dh:(h + 1) * dh] = o[:, h, :]


def _attn_sample(q, k_cache, v_cache, l, *, bs, heads):
    _, ns, nr, dh = k_cache.shape
    width = heads * dh
    kern = functools.partial(_attn_sample_kernel, heads=heads, dh=dh)
    return pl.pallas_call(
        kern,
        grid=(ns // bs,),
        in_specs=[
            pl.BlockSpec((bs, width), lambda j: (j, 0)),
            pl.BlockSpec((None, bs, nr, dh), lambda j: (l, j, 0, 0)),
            pl.BlockSpec((None, bs, nr, dh), lambda j: (l, j, 0, 0)),
        ],
        out_specs=pl.BlockSpec((bs, width), lambda j: (j, 0)),
        out_shape=jax.ShapeDtypeStruct((ns, width), F32),
        compiler_params=_cparams(("parallel",)),
        name="attn_sample",
    )(q, k_cache, v_cache)


def _ffn_kernel(x_hbm_ref, gain_ref, wu_ref, wd_ref, *refs, final_rows):
    if final_rows is None:
        o_ref, hn_ref, sem = refs
    else:
        fgain_ref, o_ref, ys_ref, hn_ref, sem = refs
    tm = o_ref.shape[0]
    n_chunk = sem.shape[0]
    rc = tm // n_chunk

    @pl.when(pl.program_id(1) == 0)
    def _():
        r0 = pl.multiple_of(pl.program_id(0) * tm, tm)
        copies = [pltpu.make_async_copy(x_hbm_ref.at[pl.ds(r0 + c * rc, rc), :],
                                        o_ref.at[pl.ds(c * rc, rc), :], sem.at[c])
                  for c in range(n_chunk)]
        for cp in copies:
            cp.start()
        gain = gain_ref[...]
        for c, cp in enumerate(copies):
            cp.wait()
            rows_c = slice(c * rc, (c + 1) * rc)
            hn_ref[rows_c, :] = _rms_rows(o_ref[rows_c, :], gain).astype(BF16)

    wu = wu_ref[...].astype(BF16)
    hm = tm // 2
    h = jnp.concatenate(
        [jnp.dot(hn_ref[0:hm, :], wu, preferred_element_type=F32),
         jnp.dot(hn_ref[hm:tm, :], wu, preferred_element_type=F32)], axis=0)
    h = jnp.square(jnp.maximum(h, 0.0)).astype(BF16)
    o_ref[...] += jnp.dot(h, wd_ref[...].astype(BF16), preferred_element_type=F32)

    if final_rows is not None:
        @pl.when(pl.program_id(1) == pl.num_programs(1) - 1)
        def _():
            _norm_rows_to(o_ref, fgain_ref, o_ref, tm)
            ns = ys_ref.shape[0]
            tail = final_rows % tm
            ys_ref[...] = o_ref[tail:tail + ns, :]


def _ffn(x, gain, w_up, w_down, l, *, tm, tf, final_gain=None, final_rows=None):
    rows, d = x.shape
    dff = w_up.shape[2]
    n_tiles = rows // tm
    w_specs = [
        pl.BlockSpec((None, d, tf), lambda i, k: (l, 0, k)),
        pl.BlockSpec((None, tf, d), lambda i, k: (l, k, 0)),
    ]
    gain_spec = pl.BlockSpec((None, 1, d), lambda i, k: (l, 0, 0))
    tile_spec = pl.BlockSpec((tm, d), lambda i, k: (i, 0))
    if final_gain is None:
        in_specs = [pl.BlockSpec(memory_space=pl.ANY), gain_spec] + w_specs
        out_specs = tile_spec
        out_shape = jax.ShapeDtypeStruct((rows, d), F32)
        args = (x, gain, w_up, w_down)
    else:
        ns = rows - final_rows
        assert final_rows > (n_tiles - 1) * tm and final_rows % tm + ns == tm
        in_specs = ([pl.BlockSpec(memory_space=pl.ANY), gain_spec] + w_specs
                    + [pl.BlockSpec((1, d), lambda i, k: (0, 0))])
        out_specs = [tile_spec, pl.BlockSpec((ns, d), lambda i, k: (0, 0))]
        out_shape = [jax.ShapeDtypeStruct((final_rows, d), F32),
                     jax.ShapeDtypeStruct((ns, d), F32)]
        args = (x, gain, w_up, w_down, final_gain)
    return pl.pallas_call(
        functools.partial(_ffn_kernel, final_rows=final_rows),
        grid=(n_tiles, dff // tf),
        in_specs=in_specs,
        out_specs=out_specs,
        out_shape=out_shape,
        scratch_shapes=[pltpu.VMEM((tm, d), BF16),
                        pltpu.SemaphoreType.DMA((tm // _pick_tile(tm, FFN_ROW_CHUNK, 16),))],
        compiler_params=pltpu.CompilerParams(
            dimension_semantics=("arbitrary", "arbitrary"),
            vmem_limit_bytes=V7X_FFN_VMEM_LIMIT_BYTES),
        name="ffn",
    )(*args)


def kernel(x_prompt, x_sample, mem_prompt, cache_conv, state_ssm_re, state_ssm_im, cache_mem_k, cache_mem_v, norm_mix_g, w_in, conv_w, conv_b, conv_ln_g, conv_ln_b, ssm_a_re, ssm_a_im, ssm_log_dt, ssm_b_re, ssm_b_im, ssm_c_re, ssm_c_im, ssm_d, w_glu, branch_g_conv, branch_g_ssm, w_out, norm_x_g, norm_mem_g, w_xq, w_xk, w_xv, w_xo, norm_ffn_g, w_up, w_down, norm_final_g):
    nb, seq, d = x_prompt.shape
    ns = x_sample.shape[0]
    depth = w_in.shape[0]
    n_mem = mem_prompt.shape[1]
    heads, dh = cache_mem_k.shape[3], cache_mem_k.shape[4]
    width = heads * dh
    ch = conv_w.shape[2]
    g_ssm, n_state = ssm_a_re.shape[1], ssm_a_re.shape[2]
    rows_p = nb * seq
    rows = rows_p + ns

    x = jnp.concatenate([x_prompt.reshape(rows_p, d), x_sample.reshape(ns, d)], axis=0)
    mem = mem_prompt.reshape(nb * n_mem, d)
    vec = lambda a: a.reshape(depth, 1, a.shape[-1])
    mix_g, x_g, mem_g, ffn_g = vec(norm_mix_g), vec(norm_x_g), vec(norm_mem_g), vec(norm_ffn_g)
    cb, lng, lnb, bgc, bgs = (vec(conv_b), vec(conv_ln_g), vec(conv_ln_b),
                              vec(branch_g_conv), vec(branch_g_ssm))
    w_in_b, w_glu_b, w_out_b = w_in.astype(BF16), w_glu.astype(BF16), w_out.astype(BF16)
    w_xq_b, w_xk_b, w_xv_b, w_xo_b = (w_xq.astype(BF16), w_xk.astype(BF16),
                                      w_xv.astype(BF16), w_xo.astype(BF16))
    sw = _ssm_weights(ssm_a_re, ssm_a_im, ssm_log_dt, ssm_b_re, ssm_b_im, ssm_c_re, ssm_c_im, ssm_d)
    st_re = state_ssm_re.reshape(depth, ns, g_ssm * n_state)
    st_im = state_ssm_im.reshape(depth, ns, g_ssm * n_state)
    k_cache = cache_mem_k.reshape(depth, ns, n_mem * heads, dh)
    v_cache = cache_mem_v.reshape(depth, ns, n_mem * heads, dh)
    cache_t = jnp.transpose(cache_conv, (0, 2, 1, 3))

    conv_wb = jnp.broadcast_to(conv_w[:, :, None, :], (depth, CONV_W, 8, ch))
    tm = _pick_tile(rows, 640, NORM_CHUNK)
    cache_out = jnp.zeros(cache_t.shape, F32)
    cn = jnp.zeros((rows, ch), BF16)
    mk_p, mv_p, cb_p, sr_p, si_p, sr_s, si_s = [], [], [], [], [], [], []
    for l in range(depth):
        g_all, s_all = _in_proj(x, mix_g, w_in_b, l, tm=tm, tn=_pick_tile(ch, 256, LANES))
        cn = _conv_prompt(g_all, conv_wb, cb, lng, lnb, bgc, cn, l, nb=nb, seq=seq)
        cache_out, cn = _conv_sample(cache_t, g_all, conv_w, cb, lng, lnb, bgc, cn, cache_out, l,
                                     row0=rows_p, bs=16)
        gy, hre_p, him_p, hre_s, him_s = _ssm(s_all, sw, st_re, st_im, l, nb=nb, seq=seq, ns=ns)
        sn = _glu_norm(gy, w_glu_b, bgs, l, tm=_pick_tile(rows, 320, NORM_CHUNK))
        x = _mix_out(cn, sn, w_out_b, x, l, tm=tm, tn=_pick_tile(d, 1024, LANES))
        k_p, v_p = _norm_proj(mem, mem_g, [w_xk_b, w_xv_b], l,
                              tm=_pick_tile(nb * n_mem, 512, NORM_CHUNK), name="mem_kv")
        x = _attn_prompt(x, x_g, w_xq_b, k_p, v_p, w_xo_b, l, nb=nb, seq=seq,
                         tq=_pick_tile(seq, 256, NORM_CHUNK), heads=heads)
        (q_s,) = _norm_proj(x, x_g, [w_xq_b], l, tm=ns, name="q_proj_sample",
                            row0=rows_p, nrows=ns)
        o_s = _attn_sample(q_s, k_cache, v_cache, l, bs=8, heads=heads)
        x = _proj_res_rows(o_s, w_xo_b, x, l, row0=rows_p,
                           tn=_pick_tile(d, 2048, LANES), name="attn_out_sample")
        ffn_tiles = dict(tm=_pick_tile(rows, 832, NORM_CHUNK),
                         tf=_pick_tile(w_up.shape[2], 256, LANES))
        if l < depth - 1:
            x = _ffn(x, ffn_g, w_up, w_down, l, **ffn_tiles)
        else:
            y_prompt, y_sample = _ffn(x, ffn_g, w_up, w_down, l, **ffn_tiles,
                                      final_gain=norm_final_g.reshape(1, d), final_rows=rows_p)

        mk_p.append(k_p.reshape(nb, n_mem, heads, dh))
        mv_p.append(v_p.reshape(nb, n_mem, heads, dh))
        cb_p.append(jnp.stack([g_all[(b + 1) * seq - (CONV_W - 1):(b + 1) * seq]
                               for b in range(nb)]))
        sr_p.append(hre_p.reshape(nb, g_ssm, n_state))
        si_p.append(him_p.reshape(nb, g_ssm, n_state))
        sr_s.append(hre_s.reshape(ns, g_ssm, n_state))
        si_s.append(him_s.reshape(ns, g_ssm, n_state))

    return (y_prompt.reshape(nb, seq, d), y_sample.reshape(ns, 1, d), jnp.stack(mk_p), jnp.stack(mv_p), jnp.stack(cb_p),
            jnp.stack(sr_p), jnp.stack(si_p), jnp.transpose(cache_out, (0, 2, 1, 3)),
            jnp.stack(sr_s), jnp.stack(si_s))
```

```python
import functools

import jax
import jax.numpy as jnp
from jax import lax
from jax.experimental import pallas as pl
from jax.experimental.pallas import tpu as pltpu

F32 = jnp.float32
BF16 = jnp.bfloat16
EPS = 1e-6

V7X_VMEM_LIMIT_BYTES = 56 * 1024 * 1024
V7X_FFN_VMEM_LIMIT_BYTES = 60 * 1024 * 1024
LANES = 128
NORM_CHUNK = 32
FFN_ROW_CHUNK = 64
NORM_UNROLL_MAX = 16

CONV_W = 31
CONV_HALO = 32
CONV_T = 128

SSM_T = 16
SSM_GB = 8
SSM_P = 16
SSM_N = 64
SSM_SW = 2 * SSM_GB * SSM_N


def _cparams(sem):
    return pltpu.CompilerParams(dimension_semantics=sem,
                                vmem_limit_bytes=V7X_VMEM_LIMIT_BYTES)


def _rms_rows(x, g):
    ms = jnp.mean(x * x, axis=-1, keepdims=True)
    return x * lax.rsqrt(ms + EPS) * g


def _norm_rows_to(x_ref, gain_ref, hn_ref, rows):
    gain = gain_ref[...]
    small = 2 * NORM_CHUNK
    if rows % small == 0 and rows // small <= NORM_UNROLL_MAX:
        for c in range(rows // small):
            rows_c = slice(c * small, (c + 1) * small)
            hn_ref[rows_c, :] = _rms_rows(x_ref[rows_c, :], gain).astype(hn_ref.dtype)
        return
    chunk = next(c for c in (4 * NORM_CHUNK, 2 * NORM_CHUNK, NORM_CHUNK) if rows % c == 0)

    def body(c, carry):
        r = pl.multiple_of(c * chunk, chunk)
        hn_ref[pl.ds(r, chunk), :] = _rms_rows(
            x_ref[pl.ds(r, chunk), :], gain).astype(hn_ref.dtype)
        return carry

    lax.fori_loop(0, rows // chunk, body, 0)


def _pick_tile(n, target, mult):
    best = None
    for t in range(mult, min(n, target) + 1, mult):
        if n % t == 0:
            best = t
    if best is None:
        raise ValueError(f"no tile for extent {n} (multiple of {mult}, <= {target})")
    return best


def _nt_dot(a, b):
    return lax.dot_general(a, b, (((1,), (1,)), ((), ())),
                           preferred_element_type=F32)


def _in_proj_kernel(x_ref, gain_ref, wv_ref, wg_ref, ws_ref, g_ref, s_ref, hn_ref):
    @pl.when(pl.program_id(1) == 0)
    def _():
        _norm_rows_to(x_ref, gain_ref, hn_ref, x_ref.shape[0])

    hn = hn_ref[...]
    val = jnp.dot(hn, wv_ref[...], preferred_element_type=F32)
    gate = jnp.dot(hn, wg_ref[...], preferred_element_type=F32)
    g_ref[...] = val * jax.nn.sigmoid(gate)
    s_ref[...] = jnp.dot(hn, ws_ref[...], preferred_element_type=F32)


def _in_proj(x, gain, w_in, l, *, tm, tn):
    rows, d = x.shape
    ch = w_in.shape[2] // 3
    nj = ch // tn
    return pl.pallas_call(
        _in_proj_kernel,
        grid=(rows // tm, nj),
        in_specs=[
            pl.BlockSpec((tm, d), lambda i, j: (i, 0)),
            pl.BlockSpec((None, 1, d), lambda i, j: (l, 0, 0)),
            pl.BlockSpec((None, d, tn), lambda i, j: (l, 0, j)),
            pl.BlockSpec((None, d, tn), lambda i, j: (l, 0, nj + j)),
            pl.BlockSpec((None, d, tn), lambda i, j: (l, 0, 2 * nj + j)),
        ],
        out_specs=[
            pl.BlockSpec((tm, tn), lambda i, j: (i, j)),
            pl.BlockSpec((tm, tn), lambda i, j: (i, j)),
        ],
        out_shape=[jax.ShapeDtypeStruct((rows, ch), F32),
                   jax.ShapeDtypeStruct((rows, ch), F32)],
        scratch_shapes=[pltpu.VMEM((tm, d), BF16)],
        compiler_params=_cparams(("parallel", "arbitrary")),
        name="in_proj",
    )(x, gain, w_in, w_in, w_in)


def _conv_tail(c, b_ref, lng_ref, lnb_ref, bg_ref):
    c = c + b_ref[...]
    xc = c - jnp.mean(c, axis=-1, keepdims=True)
    var = jnp.mean(xc * xc, axis=-1, keepdims=True)
    y = xc * lax.rsqrt(var + EPS) * lng_ref[...] + lnb_ref[...]
    y = y * jax.nn.sigmoid(y)
    return _rms_rows(y, bg_ref[...])


def _conv_prompt_kernel(cur_ref, prev_ref, wb_ref, b_ref, lng_ref, lnb_ref, bg_ref, dst_any_ref,
                        o_ref, ext_ref, c_ref):
    del dst_any_ref
    t, ch = cur_ref.shape
    rows_e = CONV_HALO + t
    first = pl.program_id(1) == 0
    ext_ref[0, 0:CONV_HALO, :] = jnp.where(first, 0.0, prev_ref[...])
    ext_ref[0, CONV_HALO:rows_e, :] = cur_ref[...]
    for r in range(1, 8):
        ext_ref[r] = pltpu.roll(ext_ref[0], rows_e - r, axis=0)
    off = CONV_HALO - (CONV_W - 1)
    n_rb = t // 8
    for lc in range(ch // LANES):
        lanes = slice(lc * LANES, (lc + 1) * LANES)
        acc = [None] * n_rb
        for r in range(8):
            taps = [(k, (off + k) // 8) for k in range(CONV_W) if (off + k) % 8 == r]
            w = {k: wb_ref[k, :, lanes] for k, _ in taps}
            for m in range(rows_e // 8):
                uses = [(k, m - a) for k, a in taps if 0 <= m - a < n_rb]
                if not uses:
                    continue
                v = ext_ref[r, m * 8:(m + 1) * 8, lanes]
                for k, rb in uses:
                    term = w[k] * v
                    acc[rb] = term if acc[rb] is None else acc[rb] + term
        for rb in range(n_rb):
            c_ref[rb * 8:(rb + 1) * 8, lanes] = acc[rb]
    o_ref[...] = _conv_tail(c_ref[...], b_ref, lng_ref, lnb_ref, bg_ref).astype(o_ref.dtype)


def _conv_prompt(g_all, conv_wb, conv_b, ln_g, ln_b, bg, dst, l, *, nb, seq):
    rows, ch = g_all.shape
    t = CONV_T
    per_seq = seq // t
    vec = lambda: pl.BlockSpec((None, 1, ch), lambda b, i: (l, 0, 0))
    return pl.pallas_call(
        _conv_prompt_kernel,
        grid=(nb, per_seq),
        in_specs=[
            pl.BlockSpec((t, ch), lambda b, i: (b * per_seq + i, 0)),
            pl.BlockSpec((CONV_HALO, ch),
                         lambda b, i: (jnp.maximum((b * seq + i * t) // CONV_HALO - 1, 0), 0)),
            pl.BlockSpec((None, CONV_W, 8, ch), lambda b, i: (l, 0, 0, 0)),
            vec(), vec(), vec(), vec(),
            pl.BlockSpec(memory_space=pl.ANY),
        ],
        out_specs=pl.BlockSpec((t, ch), lambda b, i: (b * per_seq + i, 0)),
        out_shape=jax.ShapeDtypeStruct(dst.shape, dst.dtype),
        input_output_aliases={7: 0},
        scratch_shapes=[pltpu.VMEM((8, CONV_HALO + t, ch), F32), pltpu.VMEM((t, ch), F32)],
        compiler_params=_cparams(("parallel", "arbitrary")),
        name="conv_prompt",
    )(g_all, g_all, conv_wb, conv_b, ln_g, ln_b, bg, dst)


def _conv_sample_kernel(cache_ref, g_ref, w_ref, b_ref, lng_ref, lnb_ref, bg_ref, *refs):
    cache_out_ref, cn_ref = refs[-2:]
    hist = CONV_W - 1
    g = g_ref[...]
    c = w_ref[hist:CONV_W, :] * g
    for k in range(hist):
        c = c + w_ref[k:k + 1, :] * cache_ref[k]
    cn_ref[...] = _conv_tail(c, b_ref, lng_ref, lnb_ref, bg_ref).astype(cn_ref.dtype)
    cache_out_ref[0:hist - 1] = cache_ref[1:hist]
    cache_out_ref[hist - 1] = g


def _conv_sample(cache_t, g_all, conv_w, conv_b, ln_g, ln_b, bg, cn_all, cache_out, l, *, row0, bs):
    _, hist, ns, ch = cache_t.shape
    vec = lambda: pl.BlockSpec((None, 1, ch), lambda j: (l, 0, 0))
    return pl.pallas_call(
        _conv_sample_kernel,
        grid=(ns // bs,),
        in_specs=[
            pl.BlockSpec((None, hist, bs, ch), lambda j: (l, 0, j, 0)),
            pl.BlockSpec((bs, ch), lambda j: (row0 // bs + j, 0)),
            pl.BlockSpec((None, CONV_W, ch), lambda j: (l, 0, 0)),
            vec(), vec(), vec(), vec(),
            pl.BlockSpec(memory_space=pl.ANY),
            pl.BlockSpec(memory_space=pl.ANY),
        ],
        out_specs=[
            pl.BlockSpec((None, hist, bs, ch), lambda j: (l, 0, j, 0)),
            pl.BlockSpec((bs, ch), lambda j: (row0 // bs + j, 0)),
        ],
        out_shape=[jax.ShapeDtypeStruct(cache_t.shape, F32),
                   jax.ShapeDtypeStruct(cn_all.shape, cn_all.dtype)],
        input_output_aliases={7: 1, 8: 0},
        compiler_params=_cparams(("parallel",)),
        name="conv_sample",
    )(cache_t, g_all, conv_w, conv_b, ln_g, ln_b, bg, cn_all, cache_out)


def _ssm_kernel(u_ref, are_ref, aim_ref, ldt_ref, bre_ref, bim_ref, cre_ref, cim_ref, d_ref,
                hre_in_ref, him_in_ref,
                gy_ref, hre_p_ref, him_p_ref, hre_s_ref, him_s_ref,
                ws_ref, wit_ref, dst_ref, x_ref, s_ref, hp_ref, hpb_ref, *, nb, nc, ns):
    t_len = SSM_T
    half = SSM_SW // 2
    rows_c = nb * nc
    row_s = rows_c * t_len

    a_re = are_ref[...]
    a_im = aim_ref[...]
    dt = jnp.exp(ldt_ref[...])
    mag = jnp.exp(a_re * dt)
    ang = a_im * dt
    abar_r = mag * jnp.cos(ang)
    abar_i = mag * jnp.sin(ang)
    nr = abar_r - 1.0
    ni = abar_i
    den = a_re * a_re + a_im * a_im
    cf_r = (nr * a_re + ni * a_im) / den
    cf_i = (ni * a_re - nr * a_im) / den

    pw = [(jnp.ones_like(abar_r), jnp.zeros_like(abar_r))]
    for _ in range(t_len):
        pr, pi = pw[-1]
        pw.append((pr * abar_r - pi * abar_i, pr * abar_i + pi * abar_r))

    gp = bre_ref.shape[0]
    own = (lax.broadcasted_iota(jnp.int32, (gp, half), 0) // SSM_P
           == lax.broadcasted_iota(jnp.int32, (gp, half), 1) // SSM_N)

    def expand(m_ref):
        wide = jnp.concatenate([m_ref[...]] * (half // LANES), axis=1)
        return jnp.where(own, wide, 0.0)

    bre, bim, cre, cim = expand(bre_ref), expand(bim_ref), expand(cre_ref), expand(cim_ref)
    b1 = jnp.concatenate([bre, bim], axis=1)
    b2 = jnp.concatenate([-bim, bre], axis=1)
    c1 = jnp.concatenate([cre, -cim], axis=1)
    c2 = jnp.concatenate([-cim, -cre], axis=1)
    c1_hi = c1.astype(BF16)
    c1_lo = (c1 - c1_hi.astype(F32)).astype(BF16)
    dst_ref[(t_len - 1) * LANES:, 0:LANES] = jnp.zeros((LANES, LANES), BF16)
    for t in range(t_len):
        sl = slice(t * LANES, (t + 1) * LANES)
        pr, pi = pw[t_len - 1 - t]
        acr = pr * cf_r - pi * cf_i
        aci = pr * cf_i + pi * cf_r
        ws_t = acr * b1 + aci * b2
        ws_hi = ws_t.astype(BF16)
        ws_lo = (ws_t - ws_hi.astype(F32)).astype(BF16)
        ws_ref[sl, :] = ws_hi
        lag = (_nt_dot(ws_hi, c1_hi) + _nt_dot(ws_hi, c1_lo) + _nt_dot(ws_lo, c1_hi)).astype(BF16)
        dst_ref[sl, LANES:] = lag
        if t > 0:
            dst_ref[(t - 1) * LANES:t * LANES, 0:LANES] = lag
        pr, pi = pw[t + 1]
        wit_ref[sl, :] = (pr * c1 + pi * c2).astype(BF16)
        x_ref[:, sl] = u_ref[pl.ds(t, rows_c, stride=t_len), :].astype(BF16)

    s_ref[...] = jnp.dot(x_ref[...], ws_ref[...], preferred_element_type=F32)
    at_r = pw[t_len][0][:, :half]
    at_i = pw[t_len][1][:, :half]

    def scan_body(c, carry):
        new = []
        for b in range(nb):
            hr, hi = carry[b]
            row = pl.ds(b * nc + c, 1)
            hp_ref[row, 0:half] = hr
            hp_ref[row, half:] = hi
            sr = s_ref[row, 0:half]
            si = s_ref[row, half:]
            new.append((at_r * hr - at_i * hi + sr, at_r * hi + at_i * hr + si))
        return tuple(new)

    zero = jnp.zeros((1, half), F32)
    final = lax.fori_loop(0, nc, scan_body, tuple((zero, zero) for _ in range(nb)))
    for b in range(nb):
        hre_p_ref[b:b + 1, :] = final[b][0]
        him_p_ref[b:b + 1, :] = final[b][1]
    hpb_ref[...] = hp_ref[...].astype(BF16)

    d_skip = d_ref[...]
    for t in range(0, t_len, 2):
        y = jnp.dot(x_ref[:, 0:(t + 2) * LANES], dst_ref[(t_len - 2 - t) * LANES:, :],
                    preferred_element_type=F32)
        y = y + _nt_dot(hpb_ref[...], wit_ref[t * LANES:(t + 2) * LANES, :])
        for q in range(2):
            u_t = u_ref[pl.ds(t + q, rows_c, stride=t_len), :]
            gy_ref[pl.ds(t + q, rows_c, stride=t_len), :] = jax.nn.gelu(
                y[:, q * LANES:(q + 1) * LANES] + d_skip * u_t)

    u_s = u_ref[row_s:row_s + ns, :]
    bu = jnp.dot(u_s.astype(BF16), ws_ref[(t_len - 1) * LANES:, :], preferred_element_type=F32)
    a1_r = abar_r[:, :half]
    a1_i = abar_i[:, :half]
    hre = hre_in_ref[...]
    him = him_in_ref[...]
    nre = a1_r * hre - a1_i * him + bu[:, :half]
    nim = a1_r * him + a1_i * hre + bu[:, half:]
    hre_s_ref[...] = nre
    him_s_ref[...] = nim
    h_cat = jnp.concatenate([nre, nim], axis=1).astype(BF16)
    y_s = _nt_dot(h_cat, c1.astype(BF16))
    gy_ref[row_s:row_s + ns, :] = jax.nn.gelu(y_s + d_skip * u_s)


def _ssm_weights(a_re, a_im, log_dt, b_re, b_im, c_re, c_im, ssm_d):
    nl, g, n = a_re.shape
    p = b_re.shape[3]
    nbd = g // SSM_GB
    assert (n, p, 2 * n) == (SSM_N, SSM_P, LANES)

    def rowvec(a):
        a = a.reshape(nl, nbd, 1, SSM_GB * n)
        return jnp.concatenate([a, a], axis=-1)

    def rows_gp(m):
        m = m.reshape(nl, nbd, SSM_GB * p, n)
        return jnp.concatenate([m, m], axis=-1)

    return dict(
        are=rowvec(a_re), aim=rowvec(a_im),
        ldt=rowvec(jnp.broadcast_to(log_dt[:, :, None], (nl, g, n))),
        bre=rows_gp(b_re.reshape(nl, nbd, SSM_GB, n, p).transpose(0, 1, 2, 4, 3)),
        bim=rows_gp(b_im.reshape(nl, nbd, SSM_GB, n, p).transpose(0, 1, 2, 4, 3)),
        cre=rows_gp(c_re.reshape(nl, nbd, SSM_GB, p, n)),
        cim=rows_gp(c_im.reshape(nl, nbd, SSM_GB, p, n)),
        d=ssm_d.reshape(nl, nbd, 1, SSM_GB * p),
    )


def _ssm(s_all, sw, state_re, state_im, l, *, nb, seq, ns):
    rows, ch = s_all.shape
    nbd = ch // LANES
    nc = seq // SSM_T
    rows_c = nb * nc
    half = SSM_SW // 2
    gp = SSM_GB * SSM_P
    row = lambda w: pl.BlockSpec((None, None, 1, w), lambda j: (l, j, 0, 0))
    mat = lambda: pl.BlockSpec((None, None, gp, LANES), lambda j: (l, j, 0, 0))
    kern = functools.partial(_ssm_kernel, nb=nb, nc=nc, ns=ns)
    return pl.pallas_call(
        kern,
        grid=(nbd,),
        in_specs=[
            pl.BlockSpec((rows, LANES), lambda j: (0, j)),
            row(SSM_SW), row(SSM_SW), row(SSM_SW),
            mat(), mat(), mat(), mat(),
            row(gp),
            pl.BlockSpec((None, ns, half), lambda j: (l, 0, j)),
            pl.BlockSpec((None, ns, half), lambda j: (l, 0, j)),
        ],
        out_specs=[
            pl.BlockSpec((rows, LANES), lambda j: (0, j)),
            pl.BlockSpec((nb, half), lambda j: (0, j)),
            pl.BlockSpec((nb, half), lambda j: (0, j)),
            pl.BlockSpec((ns, half), lambda j: (0, j)),
            pl.BlockSpec((ns, half), lambda j: (0, j)),
        ],
        out_shape=[
            jax.ShapeDtypeStruct((rows, ch), F32),
            jax.ShapeDtypeStruct((nb, nbd * half), F32),
            jax.ShapeDtypeStruct((nb, nbd * half), F32),
            jax.ShapeDtypeStruct((ns, nbd * half), F32),
            jax.ShapeDtypeStruct((ns, nbd * half), F32),
        ],
        scratch_shapes=[
            pltpu.VMEM((SSM_T * LANES, SSM_SW), BF16),
            pltpu.VMEM((SSM_T * LANES, SSM_SW), BF16),
            pltpu.VMEM((SSM_T * LANES, 2 * LANES), BF16),
            pltpu.VMEM((rows_c, SSM_T * LANES), BF16),
            pltpu.VMEM((rows_c, SSM_SW), F32),
            pltpu.VMEM((rows_c, SSM_SW), F32),
            pltpu.VMEM((rows_c, SSM_SW), BF16),
        ],
        compiler_params=_cparams(("parallel",)),
        name="ssm",
    )(s_all, sw["are"], sw["aim"], sw["ldt"], sw["bre"], sw["bim"], sw["cre"], sw["cim"], sw["d"],
      state_re, state_im)


def _glu_norm_kernel(gy_ref, w_ref, bg_ref, o_ref):
    gy = gy_ref[...]
    z = jnp.dot(gy.astype(BF16), w_ref[...], preferred_element_type=F32)
    o_ref[...] = _rms_rows(gy * jax.nn.sigmoid(z), bg_ref[...]).astype(o_ref.dtype)


def _glu_norm(gy, w_glu, bg, l, *, tm):
    rows, ch = gy.shape
    return pl.pallas_call(
        _glu_norm_kernel,
        grid=(rows // tm,),
        in_specs=[
            pl.BlockSpec((tm, ch), lambda i: (i, 0)),
            pl.BlockSpec((None, ch, ch), lambda i: (l, 0, 0)),
            pl.BlockSpec((None, 1, ch), lambda i: (l, 0, 0)),
        ],
        out_specs=pl.BlockSpec((tm, ch), lambda i: (i, 0)),
        out_shape=jax.ShapeDtypeStruct((rows, ch), BF16),
        compiler_params=_cparams(("parallel",)),
        name="glu_norm",
    )(gy, w_glu, bg)


def _mix_out_kernel(c_ref, s_ref, wc_ref, ws_ref, x_ref, o_ref):
    acc = jnp.dot(c_ref[...], wc_ref[...], preferred_element_type=F32)
    acc = acc + jnp.dot(s_ref[...], ws_ref[...], preferred_element_type=F32)
    o_ref[...] = x_ref[...] + acc


def _mix_out(cn, sn, w_out, x, l, *, tm, tn):
    rows, d = x.shape
    ch = cn.shape[1]
    return pl.pallas_call(
        _mix_out_kernel,
        grid=(rows // tm, d // tn),
        in_specs=[
            pl.BlockSpec((tm, ch), lambda i, j: (i, 0)),
            pl.BlockSpec((tm, ch), lambda i, j: (i, 0)),
            pl.BlockSpec((None, ch, tn), lambda i, j: (l, 0, j)),
            pl.BlockSpec((None, ch, tn), lambda i, j: (l, 1, j)),
            pl.BlockSpec((tm, tn), lambda i, j: (i, j)),
        ],
        out_specs=pl.BlockSpec((tm, tn), lambda i, j: (i, j)),
        out_shape=jax.ShapeDtypeStruct((rows, d), F32),
        compiler_params=_cparams(("parallel", "arbitrary")),
        name="mix_out",
    )(cn, sn, w_out, w_out, x)


def _norm_proj_kernel(x_ref, gain_ref, *refs):
    nw = (len(refs) - 1) // 2
    w_refs, o_refs, hn_ref = refs[:nw], refs[nw:2 * nw], refs[2 * nw]
    _norm_rows_to(x_ref, gain_ref, hn_ref, x_ref.shape[0])
    hn = hn_ref[...]
    for w_ref, o_ref in zip(w_refs, o_refs):
        o_ref[...] = jnp.dot(hn, w_ref[...], preferred_element_type=F32)


def _norm_proj(x, gain, weights, l, *, tm, name, row0=0, nrows=None):
    d = x.shape[1]
    nrows = x.shape[0] - row0 if nrows is None else nrows
    n = weights[0].shape[2]
    blk0 = row0 // tm
    return pl.pallas_call(
        _norm_proj_kernel,
        grid=(nrows // tm,),
        in_specs=[pl.BlockSpec((tm, d), lambda i: (blk0 + i, 0)),
                  pl.BlockSpec((None, 1, d), lambda i: (l, 0, 0))]
                 + [pl.BlockSpec((None, d, n), lambda i: (l, 0, 0)) for _ in weights],
        out_specs=[pl.BlockSpec((tm, n), lambda i: (i, 0)) for _ in weights],
        out_shape=[jax.ShapeDtypeStruct((nrows, n), F32) for _ in weights],
        scratch_shapes=[pltpu.VMEM((tm, d), BF16)],
        compiler_params=_cparams(("parallel",)),
        name=name,
    )(x, gain, *weights)


def _proj_res_kernel(a_ref, w_ref, x_ref, o_ref):
    o_ref[...] = x_ref[...] + jnp.dot(a_ref[...].astype(BF16), w_ref[...],
                                      preferred_element_type=F32)


def _proj_res_rows(a, w, x, l, *, row0, tn, name):
    d = x.shape[1]
    m, k = a.shape
    blk0 = row0 // m
    return pl.pallas_call(
        _proj_res_kernel,
        grid=(d // tn,),
        in_specs=[
            pl.BlockSpec((m, k), lambda j: (0, 0)),
            pl.BlockSpec((None, k, tn), lambda j: (l, 0, j)),
            pl.BlockSpec((m, tn), lambda j: (blk0, j)),
        ],
        out_specs=pl.BlockSpec((m, tn), lambda j: (blk0, j)),
        out_shape=jax.ShapeDtypeStruct(x.shape, x.dtype),
        input_output_aliases={2: 0},
        compiler_params=_cparams(("parallel",)),
        name=name,
    )(a, w, x)


def _softmax_rows(s):
    p = jnp.exp(s - jnp.max(s, axis=-1, keepdims=True))
    return p / jnp.sum(p, axis=-1, keepdims=True)


def _attn_prompt_kernel(x_ref, gain_ref, wq_ref, k_ref, v_ref, wo_ref, o_ref, hn_ref, oc_ref,
                        *, heads, dh):
    _norm_rows_to(x_ref, gain_ref, hn_ref, x_ref.shape[0])
    q = jnp.dot(hn_ref[...], wq_ref[...], preferred_element_type=F32)
    scale = dh ** -0.5
    for h in range(heads):
        sl = slice(h * dh, (h + 1) * dh)
        s = _nt_dot(q[:, sl].astype(BF16), k_ref[:, sl].astype(BF16)) * scale
        w = _softmax_rows(s)
        oc_ref[:, sl] = jnp.dot(w.astype(BF16), v_ref[:, sl].astype(BF16),
                                preferred_element_type=F32).astype(BF16)
    o_ref[...] = x_ref[...] + jnp.dot(oc_ref[...], wo_ref[...], preferred_element_type=F32)


def _attn_prompt(x, gain, w_xq, k, v, w_xo, l, *, nb, seq, tq, heads):
    rows, d = x.shape
    width = k.shape[1]
    n_mem = k.shape[0] // nb
    per_seq = seq // tq
    kern = functools.partial(_attn_prompt_kernel, heads=heads, dh=width // heads)
    return pl.pallas_call(
        kern,
        grid=(nb, per_seq),
        in_specs=[
            pl.BlockSpec((tq, d), lambda b, i: (b * per_seq + i, 0)),
            pl.BlockSpec((None, 1, d), lambda b, i: (l, 0, 0)),
            pl.BlockSpec((None, d, width), lambda b, i: (l, 0, 0)),
            pl.BlockSpec((n_mem, width), lambda b, i: (b, 0)),
            pl.BlockSpec((n_mem, width), lambda b, i: (b, 0)),
            pl.BlockSpec((None, width, d), lambda b, i: (l, 0, 0)),
        ],
        out_specs=pl.BlockSpec((tq, d), lambda b, i: (b * per_seq + i, 0)),
        out_shape=jax.ShapeDtypeStruct((rows, d), F32),
        input_output_aliases={0: 0},
        scratch_shapes=[pltpu.VMEM((tq, d), BF16), pltpu.VMEM((tq, width), BF16)],
        compiler_params=pltpu.CompilerParams(
            dimension_semantics=("parallel", "arbitrary"),
            vmem_limit_bytes=V7X_FFN_VMEM_LIMIT_BYTES),
        name="attn_prompt",
    )(x, gain, w_xq, k, v, w_xo)


def _attn_sample_kernel(q_ref, k_ref, v_ref, o_ref, *, heads, dh):
    bs = q_ref.shape[0]
    nr = k_ref.shape[1]
    hp = 8
    q = q_ref[...]
    head_row = lax.broadcasted_iota(jnp.int32, (bs, hp, dh), 1)
    q4 = jnp.zeros((bs, hp, dh), F32)
    for h in range(heads):
        q4 = jnp.where(head_row == h, q[:, None, h * dh:(h + 1) * dh], q4)
    s = jnp.einsum("bhd,brd->bhr", q4.astype(BF16), k_ref[...].astype(BF16),
                   preferred_element_type=F32) * (dh ** -0.5)
    own = (lax.broadcasted_iota(jnp.int32, (bs, hp, nr), 2) % heads
           == lax.broadcasted_iota(jnp.int32, (bs, hp, nr), 1))
    w = _softmax_rows(jnp.where(own, s, -1e30))
    o = jnp.einsum("bhr,brd->bhd", w.astype(BF16), v_ref[...].astype(BF16),
                   preferred_element_type=F32)
    for h in range(heads):
        o_ref[:, h * dh:(h + 1) * dh] = o[:, h, :]


def _attn_sample(q, k_cache, v_cache, l, *, bs, heads):
    _, ns, nr, dh = k_cache.shape
    width = heads * dh
    kern = functools.partial(_attn_sample_kernel, heads=heads, dh=dh)
    return pl.pallas_call(
        kern,
        grid=(ns // bs,),
        in_specs=[
            pl.BlockSpec((bs, width), lambda j: (j, 0)),
            pl.BlockSpec((None, bs, nr, dh), lambda j: (l, j, 0, 0)),
            pl.BlockSpec((None, bs, nr, dh), lambda j: (l, j, 0, 0)),
        ],
        out_specs=pl.BlockSpec((bs, width), lambda j: (j, 0)),
        out_shape=jax.ShapeDtypeStruct((ns, width), F32),
        compiler_params=_cparams(("parallel",)),
        name="attn_sample",
    )(q, k_cache, v_cache)


def _ffn_kernel(x_hbm_ref, gain_ref, wu_ref, wd_ref, *refs, final_rows):
    if final_rows is None:
        o_ref, hn_ref, sem = refs
    else:
        fgain_ref, o_ref, ys_ref, hn_ref, sem = refs
    tm = o_ref.shape[0]
    n_chunk = sem.shape[0]
    rc = tm // n_chunk

    @pl.when(pl.program_id(1) == 0)
    def _():
        r0 = pl.multiple_of(pl.program_id(0) * tm, tm)
        copies = [pltpu.make_async_copy(x_hbm_ref.at[pl.ds(r0 + c * rc, rc), :],
                                        o_ref.at[pl.ds(c * rc, rc), :], sem.at[c])
                  for c in range(n_chunk)]
        for cp in copies:
            cp.start()
        gain = gain_ref[...]
        for c, cp in enumerate(copies):
            cp.wait()
            rows_c = slice(c * rc, (c + 1) * rc)
            hn_ref[rows_c, :] = _rms_rows(o_ref[rows_c, :], gain).astype(BF16)

    wu = wu_ref[...].astype(BF16)
    hm = tm // 2
    h = jnp.concatenate(
        [jnp.dot(hn_ref[0:hm, :], wu, preferred_element_type=F32),
         jnp.dot(hn_ref[hm:tm, :], wu, preferred_element_type=F32)], axis=0)
    h = jnp.square(jnp.maximum(h, 0.0)).astype(BF16)
    o_ref[...] += jnp.dot(h, wd_ref[...].astype(BF16), preferred_element_type=F32)

    if final_rows is not None:
        @pl.when(pl.program_id(1) == pl.num_programs(1) - 1)
        def _():
            _norm_rows_to(o_ref, fgain_ref, o_ref, tm)
            ns = ys_ref.shape[0]
            tail = final_rows % tm
            ys_ref[...] = o_ref[tail:tail + ns, :]


def _ffn(x, gain, w_up, w_down, l, *, tm, tf, final_gain=None, final_rows=None):
    rows, d = x.shape
    dff = w_up.shape[2]
    n_tiles = rows // tm
    w_specs = [
        pl.BlockSpec((None, d, tf), lambda i, k: (l, 0, k)),
        pl.BlockSpec((None, tf, d), lambda i, k: (l, k, 0)),
    ]
    gain_spec = pl.BlockSpec((None, 1, d), lambda i, k: (l, 0, 0))
    tile_spec = pl.BlockSpec((tm, d), lambda i, k: (i, 0))
    if final_gain is None:
        in_specs = [pl.BlockSpec(memory_space=pl.ANY), gain_spec] + w_specs
        out_specs = tile_spec
        out_shape = jax.ShapeDtypeStruct((rows, d), F32)
        args = (x, gain, w_up, w_down)
    else:
        ns = rows - final_rows
        assert final_rows > (n_tiles - 1) * tm and final_rows % tm + ns == tm
        in_specs = ([pl.BlockSpec(memory_space=pl.ANY), gain_spec] + w_specs
                    + [pl.BlockSpec((1, d), lambda i, k: (0, 0))])
        out_specs = [tile_spec, pl.BlockSpec((ns, d), lambda i, k: (0, 0))]
        out_shape = [jax.ShapeDtypeStruct((final_rows, d), F32),
                     jax.ShapeDtypeStruct((ns, d), F32)]
        args = (x, gain, w_up, w_down, final_gain)
    return pl.pallas_call(
        functools.partial(_ffn_kernel, final_rows=final_rows),
        grid=(n_tiles, dff // tf),
        in_specs=in_specs,
        out_specs=out_specs,
        out_shape=out_shape,
        scratch_shapes=[pltpu.VMEM((tm, d), BF16),
                        pltpu.SemaphoreType.DMA((tm // _pick_tile(tm, FFN_ROW_CHUNK, 16),))],
        compiler_params=pltpu.CompilerParams(
            dimension_semantics=("arbitrary", "arbitrary"),
            vmem_limit_bytes=V7X_FFN_VMEM_LIMIT_BYTES),
        name="ffn",
    )(*args)


def kernel(x_prompt, x_sample, mem_prompt, cache_conv, state_ssm_re, state_ssm_im, cache_mem_k, cache_mem_v, norm_mix_g, w_in, conv_w, conv_b, conv_ln_g, conv_ln_b, ssm_a_re, ssm_a_im, ssm_log_dt, ssm_b_re, ssm_b_im, ssm_c_re, ssm_c_im, ssm_d, w_glu, branch_g_conv, branch_g_ssm, w_out, norm_x_g, norm_mem_g, w_xq, w_xk, w_xv, w_xo, norm_ffn_g, w_up, w_down, norm_final_g):
    nb, seq, d = x_prompt.shape
    ns = x_sample.shape[0]
    depth = w_in.shape[0]
    n_mem = mem_prompt.shape[1]
    heads, dh = cache_mem_k.shape[3], cache_mem_k.shape[4]
    width = heads * dh
    ch = conv_w.shape[2]
    g_ssm, n_state = ssm_a_re.shape[1], ssm_a_re.shape[2]
    rows_p = nb * seq
    rows = rows_p + ns

    x = jnp.concatenate([x_prompt.reshape(rows_p, d), x_sample.reshape(ns, d)], axis=0)
    mem = mem_prompt.reshape(nb * n_mem, d)
    vec = lambda a: a.reshape(depth, 1, a.shape[-1])
    mix_g, x_g, mem_g, ffn_g = vec(norm_mix_g), vec(norm_x_g), vec(norm_mem_g), vec(norm_ffn_g)
    cb, lng, lnb, bgc, bgs = (vec(conv_b), vec(conv_ln_g), vec(conv_ln_b),
                              vec(branch_g_conv), vec(branch_g_ssm))
    w_in_b, w_glu_b, w_out_b = w_in.astype(BF16), w_glu.astype(BF16), w_out.astype(BF16)
    w_xq_b, w_xk_b, w_xv_b, w_xo_b = (w_xq.astype(BF16), w_xk.astype(BF16),
                                      w_xv.astype(BF16), w_xo.astype(BF16))
    sw = _ssm_weights(ssm_a_re, ssm_a_im, ssm_log_dt, ssm_b_re, ssm_b_im, ssm_c_re, ssm_c_im, ssm_d)
    st_re = state_ssm_re.reshape(depth, ns, g_ssm * n_state)
    st_im = state_ssm_im.reshape(depth, ns, g_ssm * n_state)
    k_cache = cache_mem_k.reshape(depth, ns, n_mem * heads, dh)
    v_cache = cache_mem_v.reshape(depth, ns, n_mem * heads, dh)
    cache_t = jnp.transpose(cache_conv, (0, 2, 1, 3))

    conv_wb = jnp.broadcast_to(conv_w[:, :, None, :], (depth, CONV_W, 8, ch))
    tm = _pick_tile(rows, 640, NORM_CHUNK)
    cache_out = jnp.zeros(cache_t.shape, F32)
    cn = jnp.zeros((rows, ch), BF16)
    mk_p, mv_p, cb_p, sr_p, si_p, sr_s, si_s = [], [], [], [], [], [], []
    for l in range(depth):
        g_all, s_all = _in_proj(x, mix_g, w_in_b, l, tm=tm, tn=_pick_tile(ch, 256, LANES))
        cn = _conv_prompt(g_all, conv_wb, cb, lng, lnb, bgc, cn, l, nb=nb, seq=seq)
        cache_out, cn = _conv_sample(cache_t, g_all, conv_w, cb, lng, lnb, bgc, cn, cache_out, l,
                                     row0=rows_p, bs=16)
        gy, hre_p, him_p, hre_s, him_s = _ssm(s_all, sw, st_re, st_im, l, nb=nb, seq=seq, ns=ns)
        sn = _glu_norm(gy, w_glu_b, bgs, l, tm=_pick_tile(rows, 640, NORM_CHUNK))
        x = _mix_out(cn, sn, w_out_b, x, l, tm=tm, tn=_pick_tile(d, 1024, LANES))
        k_p, v_p = _norm_proj(mem, mem_g, [w_xk_b, w_xv_b], l,
                              tm=_pick_tile(nb * n_mem, 512, NORM_CHUNK), name="mem_kv")
        x = _attn_prompt(x, x_g, w_xq_b, k_p, v_p, w_xo_b, l, nb=nb, seq=seq,
                         tq=_pick_tile(seq, 512, NORM_CHUNK), heads=heads)
        (q_s,) = _norm_proj(x, x_g, [w_xq_b], l, tm=ns, name="q_proj_sample",
                            row0=rows_p, nrows=ns)
        o_s = _attn_sample(q_s, k_cache, v_cache, l, bs=8, heads=heads)
        x = _proj_res_rows(o_s, w_xo_b, x, l, row0=rows_p,
                           tn=_pick_tile(d, 2048, LANES), name="attn_out_sample")
        ffn_tiles = dict(tm=_pick_tile(rows, 832, NORM_CHUNK),
                         tf=_pick_tile(w_up.shape[2], 256, LANES))
        if l < depth - 1:
            x = _ffn(x, ffn_g, w_up, w_down, l, **ffn_tiles)
        else:
            y_prompt, y_sample = _ffn(x, ffn_g, w_up, w_down, l, **ffn_tiles,
                                      final_gain=norm_final_g.reshape(1, d), final_rows=rows_p)

        mk_p.append(k_p.reshape(nb, n_mem, heads, dh))
        mv_p.append(v_p.reshape(nb, n_mem, heads, dh))
        cb_p.append(jnp.stack([g_all[(b + 1) * seq - (CONV_W - 1):(b + 1) * seq]
                               for b in range(nb)]))
        sr_p.append(hre_p.reshape(nb, g_ssm, n_state))
        si_p.append(him_p.reshape(nb, g_ssm, n_state))
        sr_s.append(hre_s.reshape(ns, g_ssm, n_state))
        si_s.append(him_s.reshape(ns, g_ssm, n_state))

    return (y_prompt.reshape(nb, seq, d), y_sample.reshape(ns, 1, d), jnp.stack(mk_p), jnp.stack(mv_p), jnp.stack(cb_p),
            jnp.stack(sr_p), jnp.stack(si_p), jnp.transpose(cache_out, (0, 2, 1, 3)),
            jnp.stack(sr_s), jnp.stack(si_s))
```
